```python
import math
import jax, jax.numpy as jnp
from jax import lax
import numpy as np

D_MODEL = 2048
BATCH = 1
SEQ = 8192
DEPTH = 4

FOX_HEAD_DIM = 128
FOX_HEADS = D_MODEL // 256
FOX_WIDTH = FOX_HEADS * FOX_HEAD_DIM
DIFF_QK_DIM = 64
DIFF_V_DIM = 2 * DIFF_QK_DIM
DIFF_HEADS = D_MODEL // 256
DIFF_QK_WIDTH = 2 * DIFF_HEADS * DIFF_QK_DIM
DIFF_WIDTH = DIFF_HEADS * DIFF_V_DIM
POOL_WINDOWS = (2, 4, 8, 16)
POOL_GROUPS = len(POOL_WINDOWS)
POOL_WIDTH = D_MODEL // 2
POOL_GROUP_DIM = POOL_WIDTH // POOL_GROUPS
N_BRANCHES = 3
BLOCK_Q = 128
ROPE_THETA = 10000.0
D_FF = 11 * D_MODEL // 4
N_EXPERTS = 8
TOP_K = 2
EXPERT_D_FF = D_FF
LN_EPS = 1e-5
DEEPNORM_ALPHA = (2 * DEPTH) ** 0.25
DEEPNORM_BETA = (8 * DEPTH) ** -0.25
N_DENSE = (DEPTH + 1) // 2
N_MOE = DEPTH // 2
IN_SPLITS = (FOX_WIDTH, FOX_WIDTH, FOX_WIDTH, DIFF_QK_WIDTH, DIFF_QK_WIDTH, DIFF_WIDTH, POOL_WIDTH, N_BRANCHES * D_MODEL, FOX_HEADS)
N_IN = sum(IN_SPLITS)

kernel_name = 'hybrid_fox_pool_diffattn_moe_deepnorm'


def layer_norm(x, g, b):
    xf = x.astype(jnp.float32)
    mu = jnp.mean(xf, axis=-1, keepdims=True)
    xc = xf - mu
    var = jnp.mean(xc * xc, axis=-1, keepdims=True)
    y = xc * lax.rsqrt(var + LN_EPS) * g.astype(jnp.float32) + b.astype(jnp.float32)
    return y.astype(x.dtype)


def rms_norm(x, g):
    xf = x.astype(jnp.float32)
    y = xf * lax.rsqrt(jnp.mean(xf * xf, axis=-1, keepdims=True) + LN_EPS) * g.astype(jnp.float32)
    return y.astype(x.dtype)


def rope_tables(seq):
    pos = jnp.arange(seq, dtype=jnp.float32)
    inv = ROPE_THETA ** (-jnp.arange(0, DIFF_QK_DIM, 2, dtype=jnp.float32) / DIFF_QK_DIM)
    ang = pos[:, None] * inv[None, :]
    return jnp.cos(ang), jnp.sin(ang)


def apply_rope(x, cos, sin):
    xf = x.astype(jnp.float32)
    x1, x2 = jnp.split(xf, 2, axis=-1)
    c = cos[None, :, None, :]
    s = sin[None, :, None, :]
    return jnp.concatenate([x1 * c - x2 * s, x2 * c + x1 * s], axis=-1).astype(x.dtype)


def causal_mask(start, seq):
    qpos = start + jnp.arange(BLOCK_Q)
    kpos = jnp.arange(seq)
    return kpos[None, :] <= qpos[:, None]


def split_query_blocks(t):
    b, s = t.shape[0], t.shape[1]
    t = t.reshape((b, s // BLOCK_Q, BLOCK_Q) + t.shape[2:])
    return jnp.moveaxis(t, 1, 0)


def merge_query_blocks(t):
    t = jnp.moveaxis(t, 0, 1)
    return t.reshape((t.shape[0], -1) + t.shape[3:])


def forgetting_attention(q, k, v, log_f):
    s, dh = q.shape[1], q.shape[3]
    c = jnp.cumsum(log_f, axis=1)
    c_k = jnp.transpose(c, (0, 2, 1))
    scale = dh ** -0.5
    starts = jnp.arange(s // BLOCK_Q) * BLOCK_Q

    def block(args):
        q_blk, c_blk, start = args
        logits = jnp.einsum('bqhd,bkhd->bhqk', q_blk, k, preferred_element_type=jnp.float32) * scale
        logits = logits + jnp.transpose(c_blk, (0, 2, 1))[..., None] - c_k[:, :, None, :]
        logits = jnp.where(causal_mask(start, s), logits, -jnp.inf)
        p = jax.nn.softmax(logits, axis=-1)
        return jnp.einsum('bhqk,bkhd->bqhd', p.astype(v.dtype), v)

    out = lax.map(block, (split_query_blocks(q), split_query_blocks(c), starts))
    return merge_query_blocks(out)


def differential_attention(q1, q2, k1, k2, v, lam):
    s, dqk = q1.shape[1], q1.shape[3]
    scale = dqk ** -0.5
    starts = jnp.arange(s // BLOCK_Q) * BLOCK_Q

    def block(args):
        q1_blk, q2_blk, start = args
        mask = causal_mask(start, s)
        l1 = jnp.einsum('bqhd,bkhd->bhqk', q1_blk, k1, preferred_element_type=jnp.float32) * scale
        l2 = jnp.einsum('bqhd,bkhd->bhqk', q2_blk, k2, preferred_element_type=jnp.float32) * scale
        p1 = jax.nn.softmax(jnp.where(mask, l1, -jnp.inf), axis=-1)
        p2 = jax.nn.softmax(jnp.where(mask, l2, -jnp.inf), axis=-1)
        p = p1 - lam * p2
        return jnp.einsum('bhqk,bkhd->bqhd', p.astype(v.dtype), v)

    out = lax.map(block, (split_query_blocks(q1), split_query_blocks(q2), starts))
    return merge_query_blocks(out)


def multiscale_pool(u, w_group, scale):
    b, s = u.shape[0], u.shape[1]
    uf = u.astype(jnp.float32)
    csum = jnp.cumsum(uf, axis=1)
    t = jnp.arange(1, s + 1, dtype=jnp.float32)
    pooled = []
    for g, w in enumerate(POOL_WINDOWS):
        cg = csum[:, :, g]
        shifted = jnp.pad(cg, ((0, 0), (w, 0), (0, 0)))[:, :s]
        count = jnp.minimum(t, float(w))[None, :, None]
        pooled.append((cg - shifted) / count)
    pooled = jnp.stack(pooled, axis=2)
    delta = (pooled - uf).astype(u.dtype)
    y = jnp.einsum('bsgc,gcd->bsgd', delta, w_group) * scale.reshape(POOL_GROUPS, POOL_GROUP_DIM)
    return y.reshape(b, s, POOL_WIDTH)


def hybrid_mixer(x, w_in, b_forget, w_pool_group, pool_scale, diff_lambda, diff_norm_gain,
                 w_branch_a, w_branch_b, w_branch_c, b_gate, w_out, lambda_init, cos, sin):
    b, s, _ = x.shape
    offsets = [int(o) for o in np.cumsum(IN_SPLITS)[:-1]]
    proj = jnp.einsum('bsd,dn->bsn', x, w_in)
    fq, fk, fv, dq, dk, dv, pu, gates, fg = jnp.split(proj, offsets, axis=-1)

    log_f = jax.nn.log_sigmoid(fg.astype(jnp.float32) + b_forget.astype(jnp.float32))
    y_a = forgetting_attention(fq.reshape(b, s, FOX_HEADS, FOX_HEAD_DIM),
                               fk.reshape(b, s, FOX_HEADS, FOX_HEAD_DIM),
                               fv.reshape(b, s, FOX_HEADS, FOX_HEAD_DIM), log_f)
    y_a = y_a.reshape(b, s, FOX_WIDTH)

    y_b = multiscale_pool(pu.reshape(b, s, POOL_GROUPS, POOL_GROUP_DIM), w_pool_group, pool_scale)

    dq = apply_rope(dq.reshape(b, s, 2 * DIFF_HEADS, DIFF_QK_DIM), cos, sin).reshape(b, s, DIFF_HEADS, 2, DIFF_QK_DIM)
    dk = apply_rope(dk.reshape(b, s, 2 * DIFF_HEADS, DIFF_QK_DIM), cos, sin).reshape(b, s, DIFF_HEADS, 2, DIFF_QK_DIM)
    lp = diff_lambda.astype(jnp.float32)
    lam = jnp.exp(jnp.sum(lp[0] * lp[1])) - jnp.exp(jnp.sum(lp[2] * lp[3])) + lambda_init
    o = differential_attention(dq[:, :, :, 0], dq[:, :, :, 1], dk[:, :, :, 0], dk[:, :, :, 1],
                               dv.reshape(b, s, DIFF_HEADS, DIFF_V_DIM), lam)
    o = rms_norm(o, diff_norm_gain) * (1.0 - lambda_init)
    y_c = o.reshape(b, s, DIFF_WIDTH)

    g = jax.nn.sigmoid(gates.reshape(b, s, N_BRANCHES, D_MODEL).astype(jnp.float32)
                       + b_gate.astype(jnp.float32)).astype(x.dtype)
    h = (g[:, :, 0] * (y_a @ w_branch_a)
         + g[:, :, 1] * (y_b @ w_branch_b)
         + g[:, :, 2] * (y_c @ w_branch_c))
    return h @ w_out


def swiglu(x, w_gate, w_up, w_down):
    return (jax.nn.silu(x @ w_gate) * (x @ w_up)) @ w_down


def moe_swiglu(x, router_w, router_b, w_gate, w_up, w_down):
    logits = jnp.einsum('bsd,de->bse', x, router_w, preferred_element_type=jnp.float32) + router_b.astype(jnp.float32)
    top_v, top_i = lax.top_k(logits, TOP_K)
    top_w = jax.nn.softmax(top_v, axis=-1)
    combine = jnp.sum(jax.nn.one_hot(top_i, N_EXPERTS, dtype=jnp.float32) * top_w[..., None], axis=-2)
    y = jnp.zeros(x.shape, jnp.float32)
    for e in range(N_EXPERTS):
        y = y + combine[..., e:e + 1] * swiglu(x, w_gate[e], w_up[e], w_down[e]).astype(jnp.float32)
    return y.astype(x.dtype)


def setup_inputs(seed: int = 0) -> dict:
    key = jax.random.key(seed)
    ks = jax.random.split(key, 24)
    f32 = jnp.float32
    beta = DEEPNORM_BETA

    def nrm(k, shape, scale):
        return jax.random.normal(k, shape, f32) * scale

    col_scales = (1.0, 1.0, beta, 1.0, 1.0, beta, 1.0, 1.0, 1.0)
    col_scale = jnp.asarray(np.concatenate([np.full(n, sc, np.float32) for n, sc in zip(IN_SPLITS, col_scales)]))
    return {
        'x': nrm(ks[0], (BATCH, SEQ, D_MODEL), 1.0),
        'w_in': nrm(ks[1], (DEPTH, D_MODEL, N_IN), D_MODEL ** -0.5) * col_scale,
        'b_forget': jax.random.uniform(ks[2], (DEPTH, FOX_HEADS), f32, 1.0, 5.0),
        'w_pool_group': nrm(ks[3], (DEPTH, POOL_GROUPS, POOL_GROUP_DIM, POOL_GROUP_DIM), POOL_GROUP_DIM ** -0.5),
        'pool_scale': 1.0 + nrm(ks[4], (DEPTH, POOL_WIDTH), 0.1),
        'diff_lambda': nrm(ks[5], (DEPTH, 4, DIFF_QK_DIM), 0.1),
        'diff_norm_gain': 1.0 + nrm(ks[6], (DEPTH, DIFF_V_DIM), 0.02),
        'w_branch_a': nrm(ks[7], (DEPTH, FOX_WIDTH, D_MODEL), FOX_WIDTH ** -0.5 * beta),
        'w_branch_b': nrm(ks[8], (DEPTH, POOL_WIDTH, D_MODEL), POOL_WIDTH ** -0.5 * beta),
        'w_branch_c': nrm(ks[9], (DEPTH, DIFF_WIDTH, D_MODEL), DIFF_WIDTH ** -0.5 * beta),
        'b_gate': nrm(ks[10], (DEPTH, N_BRANCHES, D_MODEL), 0.02),
        'w_out': nrm(ks[11], (DEPTH, D_MODEL, D_MODEL), D_MODEL ** -0.5 * beta),
        'ln1_g': 1.0 + nrm(ks[12], (DEPTH, D_MODEL), 0.02),
        'ln1_b': nrm(ks[13], (DEPTH, D_MODEL), 0.02),
        'ln2_g': 1.0 + nrm(ks[14], (DEPTH, D_MODEL), 0.02),
        'ln2_b': nrm(ks[15], (DEPTH, D_MODEL), 0.02),
        'ffn_w_gate': nrm(ks[16], (N_DENSE, D_MODEL, D_FF), D_MODEL ** -0.5),
        'ffn_w_up': nrm(ks[17], (N_DENSE, D_MODEL, D_FF), D_MODEL ** -0.5 * beta),
        'ffn_w_down': nrm(ks[18], (N_DENSE, D_FF, D_MODEL), D_FF ** -0.5 * beta),
        'router_w': nrm(ks[19], (N_MOE, D_MODEL, N_EXPERTS), D_MODEL ** -0.5),
        'router_b': nrm(ks[20], (N_MOE, N_EXPERTS), 0.01),
        'expert_w_gate': nrm(ks[21], (N_MOE, N_EXPERTS, D_MODEL, EXPERT_D_FF), D_MODEL ** -0.5),
        'expert_w_up': nrm(ks[22], (N_MOE, N_EXPERTS, D_MODEL, EXPERT_D_FF), D_MODEL ** -0.5 * beta),
        'expert_w_down': nrm(ks[23], (N_MOE, N_EXPERTS, EXPERT_D_FF, D_MODEL), EXPERT_D_FF ** -0.5 * beta),
    }


def reference(x, w_in, b_forget, w_pool_group, pool_scale, diff_lambda, diff_norm_gain,
              w_branch_a, w_branch_b, w_branch_c, b_gate, w_out, ln1_g, ln1_b, ln2_g, ln2_b,
              ffn_w_gate, ffn_w_up, ffn_w_down, router_w, router_b,
              expert_w_gate, expert_w_up, expert_w_down):
    cos, sin = rope_tables(x.shape[1])
    for layer in range(DEPTH):
        lambda_init = 0.8 - 0.6 * math.exp(-0.3 * layer)
        mix = hybrid_mixer(x, w_in[layer], b_forget[layer], w_pool_group[layer], pool_scale[layer],
                           diff_lambda[layer], diff_norm_gain[layer], w_branch_a[layer], w_branch_b[layer],
                           w_branch_c[layer], b_gate[layer], w_out[layer], lambda_init, cos, sin)
        x = layer_norm(DEEPNORM_ALPHA * x + mix, ln1_g[layer], ln1_b[layer])
        j = layer // 2
        if layer % 2 == 0:
            f = swiglu(x, ffn_w_gate[j], ffn_w_up[j], ffn_w_down[j])
        else:
            f = moe_swiglu(x, router_w[j], router_b[j], expert_w_gate[j], expert_w_up[j], expert_w_down[j])
        x = layer_norm(DEEPNORM_ALPHA * x + f, ln2_g[layer], ln2_b[layer])
    return x
```

```python
import functools
import math

import jax
import jax.numpy as jnp
from jax import lax
from jax.experimental import pallas as pl
from jax.experimental.pallas import tpu as pltpu

BF16 = jnp.bfloat16
F32 = jnp.float32

D_MODEL = 2048
N_HEADS = 8
HEAD_DIM = 128
QK_DIM = 64
SEG = 1024
N_MAIN = 13 * SEG
POOL_WINDOWS = (2, 4, 8, 16)
POOL_GROUP_DIM = 256
POOL_HALO = 16
N_EXPERTS = 8
ROPE_THETA = 10000.0
LN_EPS = 1e-5
LOG2E = math.log2(math.e)
FOX_Q_SCALE = HEAD_DIM ** -0.5 * LOG2E
DIFF_Q_SCALE = QK_DIM ** -0.5 * LOG2E
LANES = 128

MIB = 2 ** 20


def _cparams(semantics, vmem_mib):
    return pltpu.CompilerParams(dimension_semantics=semantics, vmem_limit_bytes=vmem_mib * MIB)


def _rope_store(acc, cos, sin, scale, o_ref):
    lane = lax.broadcasted_iota(jnp.int32, (1, LANES), 1)
    first_half = (lane % QK_DIM) < (QK_DIM // 2)
    for c in range(SEG // LANES):
        a = acc[:, c * LANES:(c + 1) * LANES]
        partner = jnp.where(first_half, pltpu.roll(a, LANES - QK_DIM // 2, 1), pltpu.roll(a, QK_DIM // 2, 1))
        o_ref[:, c * LANES:(c + 1) * LANES] = ((a * cos + partner * sin) * scale).astype(BF16)


def _inproj_kernel(x_ref, w_ref, bg_ref, cos_ref, sin_ref, o_ref, wbf_ref):
    j = pl.program_id(0)
    i = pl.program_id(1)

    @pl.when(i == 0)
    def _cast_weights():
        wbf_ref[...] = w_ref[...].astype(BF16)

    acc = jnp.dot(x_ref[...], wbf_ref[...], preferred_element_type=F32)

    @pl.when(j == 0)
    def _fox_q():
        o_ref[...] = (acc * FOX_Q_SCALE).astype(BF16)

    @pl.when((j == 1) | (j == 2) | (j == 5) | (j == 6))
    def _plain():
        o_ref[...] = acc.astype(BF16)

    @pl.when(j == 3)
    def _diff_q():
        _rope_store(acc, cos_ref[...], sin_ref[...], DIFF_Q_SCALE, o_ref)

    @pl.when(j == 4)
    def _diff_k():
        _rope_store(acc, cos_ref[...], sin_ref[...], 1.0, o_ref)

    @pl.when(j >= 7)
    def _gates():
        o_ref[...] = jax.nn.sigmoid(acc + bg_ref[...]).astype(BF16)


def _inproj(x_bf, w_in, b_gate_flat, cos_t, sin_t, layer, tm):
    s, d = x_bf.shape
    return pl.pallas_call(
        _inproj_kernel,
        grid=(N_MAIN // SEG, s // tm),
        in_specs=[
            pl.BlockSpec((tm, d), lambda j, i: (i, 0)),
            pl.BlockSpec((None, d, SEG), lambda j, i: (layer, 0, j)),
            pl.BlockSpec((None, 1, SEG), lambda j, i: (layer, 0, jnp.maximum(j - 7, 0))),
            pl.BlockSpec((tm, LANES), lambda j, i: (i, 0)),
            pl.BlockSpec((tm, LANES), lambda j, i: (i, 0)),
        ],
        out_specs=pl.BlockSpec((tm, SEG), lambda j, i: (i, j)),
        out_shape=jax.ShapeDtypeStruct((s, N_MAIN), BF16),
        scratch_shapes=[pltpu.VMEM((d, SEG), BF16)],
        compiler_params=_cparams(("arbitrary", "arbitrary"), 48),
        name="inproj",
    )(x_bf, w_in, b_gate_flat, cos_t, sin_t)


def _split3(v):
    hi = v.astype(BF16)
    r = v - hi.astype(F32)
    mid = r.astype(BF16)
    lo = (r - mid.astype(F32)).astype(BF16)
    return hi, mid, lo


def _forget_prep_kernel(x_ref, wfg_ref, bf_ref, fq_ref, fk_ref, qa_ref, ka_ref, carry_ref, *, tm):
    i = pl.program_id(0)

    @pl.when(i == 0)
    def _init():
        carry_ref[...] = jnp.zeros_like(carry_ref)

    z = jnp.dot(x_ref[...], wfg_ref[...].astype(BF16), preferred_element_type=F32) + bf_ref[...]
    log_f = -(jnp.maximum(-z, 0.0) + jnp.log1p(jnp.exp(-jnp.abs(z)))) * LOG2E
    row = lax.broadcasted_iota(jnp.int32, (tm, tm), 0)
    col = lax.broadcasted_iota(jnp.int32, (tm, tm), 1)
    tri = (col <= row).astype(BF16)
    hi, mid, lo = _split3(log_f)
    cum = (jnp.dot(tri, hi, preferred_element_type=F32) + jnp.dot(tri, mid, preferred_element_type=F32)
           + jnp.dot(tri, lo, preferred_element_type=F32)) + carry_ref[...]
    carry_ref[...] = cum[tm - 1:tm, :]
    c_hi, c_mid, c_lo = (p.astype(F32) for p in _split3(cum))
    lane = lax.broadcasted_iota(jnp.int32, (1, LANES), 1)
    for h in range(N_HEADS):
        ch, cm, cl = c_hi[:, h:h + 1], c_mid[:, h:h + 1], c_lo[:, h:h + 1]
        q_extra = jnp.where(lane == 0, ch, jnp.where(lane == 1, cm, jnp.where(lane == 2, cl,
                  jnp.where(lane < 6, 1.0, 0.0))))
        k_extra = jnp.where(lane < 3, 1.0, jnp.where(lane == 3, -ch, jnp.where(lane == 4, -cm,
                  jnp.where(lane == 5, -cl, 0.0))))
        qa_ref[h, :, 0:HEAD_DIM] = fq_ref[:, h * HEAD_DIM:(h + 1) * HEAD_DIM]
        qa_ref[h, :, HEAD_DIM:2 * HEAD_DIM] = q_extra.astype(BF16)
        ka_ref[h, :, 0:HEAD_DIM] = fk_ref[:, h * HEAD_DIM:(h + 1) * HEAD_DIM]
        ka_ref[h, :, HEAD_DIM:2 * HEAD_DIM] = k_extra.astype(BF16)


def _forget_prep(x_bf, w_fg, b_fg, proj, tm):
    s, d = x_bf.shape
    aug = jax.ShapeDtypeStruct((N_HEADS, s, 2 * HEAD_DIM), BF16)
    return pl.pallas_call(
        functools.partial(_forget_prep_kernel, tm=tm),
        grid=(s // tm,),
        in_specs=[
            pl.BlockSpec((tm, d), lambda i: (i, 0)),
            pl.BlockSpec((d, LANES), lambda i: (0, 0)),
            pl.BlockSpec((1, LANES), lambda i: (0, 0)),
            pl.BlockSpec((tm, SEG), lambda i: (i, 0)),
            pl.BlockSpec((tm, SEG), lambda i: (i, 1)),
        ],
        out_specs=[pl.BlockSpec((N_HEADS, tm, 2 * HEAD_DIM), lambda i: (0, i, 0))] * 2,
        out_shape=[aug, aug],
        scratch_shapes=[pltpu.VMEM((1, LANES), F32)],
        compiler_params=_cparams(("arbitrary",), 32),
        name="forget_prep",
    )(x_bf, w_fg, b_fg, proj, proj)


def _causal_mask(s, q0, k0):
    row = lax.broadcasted_iota(jnp.int32, s.shape, 0) + q0
    col = lax.broadcasted_iota(jnp.int32, s.shape, 1) + k0
    return jnp.where(col <= row, s, -jnp.inf)


def _online_softmax_step(s, v, m_ref, l_ref, acc_ref):
    m_prev = m_ref[...]
    m_new = jnp.maximum(m_prev, jnp.max(s, axis=1, keepdims=True))
    alpha = jnp.exp2(m_prev - m_new)
    p = jnp.exp2(s - m_new)
    l_ref[...] = alpha * l_ref[...] + jnp.sum(p, axis=1, keepdims=True)
    acc_ref[...] = alpha * acc_ref[...] + jnp.dot(p.astype(BF16), v, preferred_element_type=F32)
    m_ref[...] = m_new


def _qk(q, k):
    return lax.dot_general(q, k, (((1,), (1,)), ((), ())), preferred_element_type=F32)


def _fox_attn_kernel(q_ref, k_ref, v_ref, o_ref, m_ref, l_ref, acc_ref, *, tq, tk):
    i = pl.program_id(1)
    q = q_ref[...]
    m_ref[...] = jnp.full_like(m_ref, -jnp.inf)
    l_ref[...] = jnp.zeros_like(l_ref)
    acc_ref[...] = jnp.zeros_like(acc_ref)

    def step(c, masked):
        k0 = pl.multiple_of(c * tk, tk)
        s = _qk(q, k_ref[pl.ds(k0, tk), :])
        if masked:
            s = _causal_mask(s, i * tq, k0)
        _online_softmax_step(s, v_ref[pl.ds(k0, tk), :], m_ref, l_ref, acc_ref)

    n_full = i * (tq // tk)

    def body(c, carry):
        step(c, False)
        return carry

    lax.fori_loop(0, n_full, body, 0)
    for d in range(tq // tk):
        step(n_full + d, True)
    o_ref[...] = (acc_ref[...] / l_ref[...]).astype(BF16)


def _fox_attn(q_aug, k_aug, proj, tq, tk):
    _, s, _ = q_aug.shape
    v_block0 = 2 * SEG // HEAD_DIM
    return pl.pallas_call(
        functools.partial(_fox_attn_kernel, tq=tq, tk=tk),
        grid=(N_HEADS, s // tq),
        in_specs=[
            pl.BlockSpec((None, tq, 2 * HEAD_DIM), lambda h, i: (h, i, 0)),
            pl.BlockSpec((None, s, 2 * HEAD_DIM), lambda h, i: (h, 0, 0)),
            pl.BlockSpec((s, HEAD_DIM), lambda h, i: (0, v_block0 + h)),
        ],
        out_specs=pl.BlockSpec((tq, HEAD_DIM), lambda h, i: (i, h)),
        out_shape=jax.ShapeDtypeStruct((s, SEG), BF16),
        scratch_shapes=[pltpu.VMEM((tq, 1), F32), pltpu.VMEM((tq, 1), F32), pltpu.VMEM((tq, HEAD_DIM), F32)],
        compiler_params=_cparams(("arbitrary", "arbitrary"), 40),
        name="fox_attn",
    )(q_aug, k_aug, proj)


def _diff_attn_kernel(q_ref, k_ref, v_ref, dl_ref, gain_ref, o_ref,
                      m1_ref, l1_ref, a1_ref, m2_ref, l2_ref, a2_ref, *, tq, tk, lambda_init):
    i = pl.program_id(1)
    q = q_ref[...]
    lane = lax.broadcasted_iota(jnp.int32, (1, HEAD_DIM), 1)
    q1 = jnp.where(lane < QK_DIM, q, jnp.zeros_like(q))
    q2 = jnp.where(lane >= QK_DIM, q, jnp.zeros_like(q))
    for m_ref, l_ref, a_ref in ((m1_ref, l1_ref, a1_ref), (m2_ref, l2_ref, a2_ref)):
        m_ref[...] = jnp.full_like(m_ref, -jnp.inf)
        l_ref[...] = jnp.zeros_like(l_ref)
        a_ref[...] = jnp.zeros_like(a_ref)

    def step(c, masked):
        k0 = pl.multiple_of(c * tk, tk)
        k = k_ref[pl.ds(k0, tk), :]
        v = v_ref[pl.ds(k0, tk), :]
        s1 = _qk(q1, k)
        s2 = _qk(q2, k)
        if masked:
            s1 = _causal_mask(s1, i * tq, k0)
            s2 = _causal_mask(s2, i * tq, k0)
        _online_softmax_step(s1, v, m1_ref, l1_ref, a1_ref)
        _online_softmax_step(s2, v, m2_ref, l2_ref, a2_ref)

    n_full = i * (tq // tk)

    def body(c, carry):
        step(c, False)
        return carry

    lax.fori_loop(0, n_full, body, 0)
    for d in range(tq // tk):
        step(n_full + d, True)

    dl = dl_ref[...]
    lam = (jnp.exp(jnp.sum(dl[0:1, :] * dl[1:2, :], axis=1, keepdims=True))
           - jnp.exp(jnp.sum(dl[2:3, :] * dl[3:4, :], axis=1, keepdims=True)) + lambda_init)
    o = a1_ref[...] / l1_ref[...] - lam * (a2_ref[...] / l2_ref[...])
    o = o * lax.rsqrt(jnp.mean(o * o, axis=1, keepdims=True) + LN_EPS) * gain_ref[...]
    o_ref[...] = (o * (1.0 - lambda_init)).astype(BF16)


def _diff_attn(proj, diff_lambda, diff_gain, layer, lambda_init, tq, tk):
    s = proj.shape[0]
    blocks_per_seg = SEG // HEAD_DIM
    vec = lambda: pltpu.VMEM((tq, 1), F32)
    acc = lambda: pltpu.VMEM((tq, HEAD_DIM), F32)
    return pl.pallas_call(
        functools.partial(_diff_attn_kernel, tq=tq, tk=tk, lambda_init=lambda_init),
        grid=(N_HEADS, s // tq),
        in_specs=[
            pl.BlockSpec((tq, HEAD_DIM), lambda h, i: (i, 3 * blocks_per_seg + h)),
            pl.BlockSpec((s, HEAD_DIM), lambda h, i: (0, 4 * blocks_per_seg + h)),
            pl.BlockSpec((s, HEAD_DIM), lambda h, i: (0, 5 * blocks_per_seg + h)),
            pl.BlockSpec((None, 4, QK_DIM), lambda h, i: (layer, 0, 0)),
            pl.BlockSpec((None, 1, HEAD_DIM), lambda h, i: (layer, 0, 0)),
        ],
        out_specs=pl.BlockSpec((tq, HEAD_DIM), lambda h, i: (i, h)),
        out_shape=jax.ShapeDtypeStruct((s, SEG), BF16),
        scratch_shapes=[vec(), vec(), acc(), vec(), vec(), acc()],
        compiler_params=_cparams(("arbitrary", "arbitrary"), 40),
        name="diff_attn",
    )(proj, proj, proj, diff_lambda, diff_gain)


def _pool_kernel(u_ref, halo_ref, wg_ref, sc_ref, o_ref, *, tm):
    i = pl.program_id(0)
    tokens_seen = (lax.broadcasted_iota(jnp.int32, (tm, 1), 0) + i * tm + 1).astype(F32)
    for g, window in enumerate(POOL_WINDOWS):
        cols = slice(g * POOL_GROUP_DIM, (g + 1) * POOL_GROUP_DIM)
        u = u_ref[:, cols].astype(F32)
        halo = halo_ref[:, cols].astype(F32)
        halo = jnp.where(i > 0, halo, jnp.zeros_like(halo))
        ext = jnp.concatenate([halo, u], axis=0)
        shift = 1
        while shift < window:
            ext = ext + pltpu.roll(ext, shift, 0)
            shift *= 2
        pooled = ext[POOL_HALO:, :] / jnp.minimum(tokens_seen, float(window))
        delta = (pooled - u).astype(BF16)
        y = jnp.dot(delta, wg_ref[g].astype(BF16), preferred_element_type=F32) * sc_ref[:, cols]
        o_ref[:, cols] = y.astype(BF16)


def _pool(proj, w_pool_group, pool_scale, layer, tm):
    s = proj.shape[0]
    n_groups = len(POOL_WINDOWS)
    halo_blocks_per_tile = tm // POOL_HALO
    return pl.pallas_call(
        functools.partial(_pool_kernel, tm=tm),
        grid=(s // tm,),
        in_specs=[
            pl.BlockSpec((tm, SEG), lambda i: (i, 6)),
            pl.BlockSpec((POOL_HALO, SEG), lambda i: (jnp.maximum(i * halo_blocks_per_tile - 1, 0), 6)),
            pl.BlockSpec((None, n_groups, POOL_GROUP_DIM, POOL_GROUP_DIM), lambda i: (layer, 0, 0, 0)),
            pl.BlockSpec((None, 1, SEG), lambda i: (layer, 0, 0)),
        ],
        out_specs=pl.BlockSpec((tm, SEG), lambda i: (i, 0)),
        out_shape=jax.ShapeDtypeStruct((s, SEG), BF16),
        compiler_params=_cparams(("arbitrary",), 32),
        name="pool",
    )(proj, proj, w_pool_group, pool_scale)


def _merge_kernel(ya_ref, yb_ref, yc_ref, wa_ref, wb_ref, wc_ref, ga_ref, gb_ref, gc_ref, o_ref,
                  wab_ref, wbb_ref, wcb_ref):
    i = pl.program_id(1)

    @pl.when(i == 0)
    def _cast_weights():
        wab_ref[...] = wa_ref[...].astype(BF16)
        wbb_ref[...] = wb_ref[...].astype(BF16)
        wcb_ref[...] = wc_ref[...].astype(BF16)

    h = ga_ref[...].astype(F32) * jnp.dot(ya_ref[...], wab_ref[...], preferred_element_type=F32)
    h = h + gb_ref[...].astype(F32) * jnp.dot(yb_ref[...], wbb_ref[...], preferred_element_type=F32)
    h = h + gc_ref[...].astype(F32) * jnp.dot(yc_ref[...], wcb_ref[...], preferred_element_type=F32)
    o_ref[...] = h.astype(BF16)


def _merge(y_a, y_b, y_c, w_a, w_b, w_c, proj, layer, tm, tn):
    s = y_a.shape[0]
    gate_block0 = 7 * SEG // tn
    gate_blocks = D_MODEL // tn
    y_spec = pl.BlockSpec((tm, SEG), lambda j, i: (i, 0))
    w_spec = pl.BlockSpec((None, SEG, tn), lambda j, i: (layer, 0, j))
    gate_spec = lambda br: pl.BlockSpec((tm, tn), lambda j, i: (i, gate_block0 + br * gate_blocks + j))
    return pl.pallas_call(
        _merge_kernel,
        grid=(D_MODEL // tn, s // tm),
        in_specs=[y_spec, y_spec, y_spec, w_spec, w_spec, w_spec, gate_spec(0), gate_spec(1), gate_spec(2)],
        out_specs=pl.BlockSpec((tm, tn), lambda j, i: (i, j)),
        out_shape=jax.ShapeDtypeStruct((s, D_MODEL), BF16),
        scratch_shapes=[pltpu.VMEM((SEG, tn), BF16)] * 3,
        compiler_params=_cparams(("arbitrary", "arbitrary"), 48),
        name="merge",
    )(y_a, y_b, y_c, w_a, w_b, w_c, proj, proj, proj)


def _matmul_kernel(x_ref, w_ref, o_ref, wbf_ref):
    @pl.when(pl.program_id(1) == 0)
    def _cast_weights():
        wbf_ref[...] = w_ref[...].astype(BF16)

    o_ref[...] = jnp.dot(x_ref[...], wbf_ref[...], preferred_element_type=F32).astype(o_ref.dtype)


def _matmul(x_bf, w, layer, tm, tn, out_dtype):
    s, k = x_bf.shape
    n = w.shape[-1]
    return pl.pallas_call(
        _matmul_kernel,
        grid=(n // tn, s // tm),
        in_specs=[pl.BlockSpec((tm, k), lambda j, i: (i, 0)),
                  pl.BlockSpec((None, k, tn), lambda j, i: (layer, 0, j))],
        out_specs=pl.BlockSpec((tm, tn), lambda j, i: (i, j)),
        out_shape=jax.ShapeDtypeStruct((s, n), out_dtype),
        scratch_shapes=[pltpu.VMEM((k, tn), BF16)],
        compiler_params=_cparams(("arbitrary", "arbitrary"), 48),
        name="matmul",
    )(x_bf, w)


def _layer_norm_rows(z, g, b):
    mu = jnp.mean(z, axis=1, keepdims=True)
    zc = z - mu
    var = jnp.mean(zc * zc, axis=1, keepdims=True)
    return zc * lax.rsqrt(var + LN_EPS) * g + b


def _add_ln_kernel(x_ref, f_ref, g_ref, b_ref, xo_ref, xb_ref, *, alpha):
    y = _layer_norm_rows(alpha * x_ref[...] + f_ref[...], g_ref[...], b_ref[...])
    xo_ref[...] = y
    xb_ref[...] = y.astype(BF16)


def _add_ln(x, f, g, b, layer, alpha, tm):
    s, d = x.shape
    row_spec = pl.BlockSpec((tm, d), lambda i: (i, 0))
    par_spec = pl.BlockSpec((None, 1, d), lambda i: (layer, 0, 0))
    return pl.pallas_call(
        functools.partial(_add_ln_kernel, alpha=alpha),
        grid=(s // tm,),
        in_specs=[row_spec, row_spec, par_spec, par_spec],
        out_specs=[row_spec, row_spec],
        out_shape=[jax.ShapeDtypeStruct((s, d), F32), jax.ShapeDtypeStruct((s, d), BF16)],
        compiler_params=_cparams(("arbitrary",), 32),
        name="add_ln",
    )(x, f, g, b)


def _gate_up_kernel(te_ref, tf_ref, nu_ref, x_ref, wg_ref, wu_ref, o_ref, wgb_ref, wub_ref):
    t = pl.program_id(1)

    @pl.when(tf_ref[t] == 1)
    def _cast_weights():
        wgb_ref[...] = wg_ref[...].astype(BF16)
        wub_ref[...] = wu_ref[...].astype(BF16)

    @pl.when(t < nu_ref[0])
    def _compute():
        x = x_ref[...]
        g = jnp.dot(x, wgb_ref[...], preferred_element_type=F32)
        u = jnp.dot(x, wub_ref[...], preferred_element_type=F32)
        o_ref[...] = (g * jax.nn.sigmoid(g) * u).astype(BF16)

    @pl.when(t >= nu_ref[0])
    def _unused_tile():
        o_ref[...] = jnp.zeros_like(o_ref)


def _gate_up(groups, x_sorted, w_gate, w_up, layer, tm, tn):
    tile_expert, tile_first, n_used = groups
    n_slots, k = x_sorted.shape
    n = w_gate.shape[-1]
    row = lambda j, t, te, tf, nu: (jnp.minimum(t, nu[0] - 1), 0)
    w_spec = pl.BlockSpec((None, None, k, tn), lambda j, t, te, tf, nu: (layer, te[t], 0, j))
    return pl.pallas_call(
        _gate_up_kernel,
        grid_spec=pltpu.PrefetchScalarGridSpec(
            num_scalar_prefetch=3,
            grid=(n // tn, n_slots // tm),
            in_specs=[pl.BlockSpec((tm, k), row), w_spec, w_spec],
            out_specs=pl.BlockSpec((tm, tn), lambda j, t, te, tf, nu: (t, j)),
            scratch_shapes=[pltpu.VMEM((k, tn), BF16)] * 2,
        ),
        out_shape=jax.ShapeDtypeStruct((n_slots, n), BF16),
        compiler_params=_cparams(("arbitrary", "arbitrary"), 48),
        name="ffn_gate_up",
    )(tile_expert, tile_first, n_used, x_sorted, w_gate, w_up)


def _down_kernel(te_ref, tf_ref, nu_ref, h_ref, w_ref, o_ref, wbf_ref):
    t = pl.program_id(1)

    @pl.when(tf_ref[t] == 1)
    def _cast_weights():
        wbf_ref[...] = w_ref[...].astype(BF16)

    @pl.when(t < nu_ref[0])
    def _compute():
        o_ref[...] = jnp.dot(h_ref[...], wbf_ref[...], preferred_element_type=F32)

    @pl.when(t >= nu_ref[0])
    def _unused_tile():
        o_ref[...] = jnp.zeros_like(o_ref)


def _down(groups, h_sorted, w_down, layer, tm, tn):
    tile_expert, tile_first, n_used = groups
    n_slots, k = h_sorted.shape
    n = w_down.shape[-1]
    return pl.pallas_call(
        _down_kernel,
        grid_spec=pltpu.PrefetchScalarGridSpec(
            num_scalar_prefetch=3,
            grid=(n // tn, n_slots // tm),
            in_specs=[pl.BlockSpec((tm, k), lambda j, t, te, tf, nu: (jnp.minimum(t, nu[0] - 1), 0)),
                      pl.BlockSpec((None, None, k, tn), lambda j, t, te, tf, nu: (layer, te[t], 0, j))],
            out_specs=pl.BlockSpec((tm, tn), lambda j, t, te, tf, nu: (t, j)),
            scratch_shapes=[pltpu.VMEM((k, tn), BF16)],
        ),
        out_shape=jax.ShapeDtypeStruct((n_slots, n), F32),
        compiler_params=_cparams(("arbitrary", "arbitrary"), 52),
        name="ffn_down",
    )(tile_expert, tile_first, n_used, h_sorted, w_down)


def _single_group(n_tiles):
    return (jnp.zeros((n_tiles,), jnp.int32),
            jnp.zeros((n_tiles,), jnp.int32).at[0].set(1),
            jnp.full((1,), n_tiles, jnp.int32))


def _router_kernel(x_ref, rw_ref, rb_ref, o_ref, cnt_ref, carry_ref, *, tm):
    i = pl.program_id(0)

    @pl.when(i == 0)
    def _init():
        carry_ref[...] = jnp.zeros_like(carry_ref)

    logits = jnp.dot(x_ref[...], rw_ref[...], preferred_element_type=F32,
                     precision=lax.Precision.HIGHEST) + rb_ref[...]
    lane = lax.broadcasted_iota(jnp.int32, (tm, LANES), 1).astype(F32)
    lg = jnp.where(lane < N_EXPERTS, logits, -jnp.inf)
    v1 = jnp.max(lg, axis=1, keepdims=True)
    e1 = jnp.min(jnp.where(lg == v1, lane, float(LANES)), axis=1, keepdims=True)
    lg2 = jnp.where(lane == e1, -jnp.inf, lg)
    v2 = jnp.max(lg2, axis=1, keepdims=True)
    e2 = jnp.min(jnp.where(lg2 == v2, lane, float(LANES)), axis=1, keepdims=True)
    t2 = jnp.exp(v2 - v1)
    w1 = 1.0 / (1.0 + t2)
    w2 = t2 / (1.0 + t2)
    pick1 = lane == e1
    pick2 = lane == e2
    onehot = jnp.where(pick1 | pick2, 1.0, 0.0)
    row = lax.broadcasted_iota(jnp.int32, (tm, tm), 0)
    col = lax.broadcasted_iota(jnp.int32, (tm, tm), 1)
    before = (col < row).astype(BF16)
    seen = jnp.dot(before, onehot.astype(BF16), preferred_element_type=F32) + carry_ref[...]
    r1 = jnp.sum(jnp.where(pick1, seen, 0.0), axis=1, keepdims=True)
    r2 = jnp.sum(jnp.where(pick2, seen, 0.0), axis=1, keepdims=True)
    carry_ref[...] = carry_ref[...] + jnp.sum(onehot, axis=0, keepdims=True)
    o_ref[...] = jnp.where(lane == 0, e1, jnp.where(lane == 1, e2, jnp.where(lane == 2, w1,
                 jnp.where(lane == 3, w2, jnp.where(lane == 4, r1, jnp.where(lane == 5, r2, 0.0))))))
    cnt_ref[...] = carry_ref[...]


def _router(x, rw_pad, rb_pad, tm):
    s, d = x.shape
    return pl.pallas_call(
        functools.partial(_router_kernel, tm=tm),
        grid=(s // tm,),
        in_specs=[pl.BlockSpec((tm, d), lambda i: (i, 0)),
                  pl.BlockSpec((d, LANES), lambda i: (0, 0)),
                  pl.BlockSpec((1, LANES), lambda i: (0, 0))],
        out_specs=[pl.BlockSpec((tm, LANES), lambda i: (i, 0)), pl.BlockSpec((1, LANES), lambda i: (0, 0))],
        out_shape=[jax.ShapeDtypeStruct((s, LANES), F32), jax.ShapeDtypeStruct((1, LANES), F32)],
        scratch_shapes=[pltpu.VMEM((1, LANES), F32)],
        compiler_params=_cparams(("arbitrary",), 32),
        name="router",
    )(x, rw_pad, rb_pad)


def _slot_token_kernel(pos_ref, tok_ref, *, n_assign, n_slots):
    def clear(sl, carry):
        tok_ref[sl] = 0
        return carry

    lax.fori_loop(0, n_slots, clear, 0)

    def place(a, carry):
        tok_ref[pos_ref[a]] = lax.shift_right_logical(a, 1)
        return carry

    lax.fori_loop(0, n_assign, place, 0)


def _slot_token(pos_flat, n_slots):
    n_assign = pos_flat.shape[0]
    return pl.pallas_call(
        functools.partial(_slot_token_kernel, n_assign=n_assign, n_slots=n_slots),
        in_specs=[pl.BlockSpec(memory_space=pltpu.SMEM)],
        out_specs=pl.BlockSpec(memory_space=pltpu.SMEM),
        out_shape=jax.ShapeDtypeStruct((n_slots,), jnp.int32),
        name="slot_token",
    )(pos_flat)


def _row_copy(src_hbm, src_row, buf, dst_row, sem):
    return pltpu.make_async_copy(src_hbm.at[pl.ds(src_row, 1), :], buf.at[pl.ds(dst_row, 1), :], sem)


def _dispatch_kernel(tok_ref, x_hbm, o_ref, buf, sem, *, tg):
    base = pl.program_id(0) * tg

    def issue(r, carry):
        _row_copy(x_hbm, tok_ref[base + r], buf, r, sem).start()
        return carry

    lax.fori_loop(0, tg, issue, 0)

    def drain(r, carry):
        _row_copy(x_hbm, 0, buf, r, sem).wait()
        return carry

    lax.fori_loop(0, tg, drain, 0)
    o_ref[...] = buf[...].astype(BF16)


def _dispatch(slot_token, x, tg):
    s, d = x.shape
    n_slots = slot_token.shape[0]
    return pl.pallas_call(
        functools.partial(_dispatch_kernel, tg=tg),
        grid_spec=pltpu.PrefetchScalarGridSpec(
            num_scalar_prefetch=1,
            grid=(n_slots // tg,),
            in_specs=[pl.BlockSpec(memory_space=pl.ANY)],
            out_specs=pl.BlockSpec((tg, d), lambda i, tok: (i, 0)),
            scratch_shapes=[pltpu.VMEM((tg, d), F32), pltpu.SemaphoreType.DMA],
        ),
        out_shape=jax.ShapeDtypeStruct((n_slots, d), BF16),
        compiler_params=_cparams(("arbitrary",), 32),
        name="moe_dispatch",
    )(slot_token, x)


def _combine_ln_kernel(p1_ref, p2_ref, y_hbm, x_ref, route_ref, g_ref, b_ref, xo_ref, xb_ref,
                       buf1, buf2, sem, *, tc, alpha):
    base = pl.program_id(0) * tc

    def issue(r, carry):
        _row_copy(y_hbm, p1_ref[base + r], buf1, r, sem).start()
        _row_copy(y_hbm, p2_ref[base + r], buf2, r, sem).start()
        return carry

    lax.fori_loop(0, tc, issue, 0)

    def drain(r, carry):
        _row_copy(y_hbm, 0, buf1, r, sem).wait()
        _row_copy(y_hbm, 0, buf2, r, sem).wait()
        return carry

    lax.fori_loop(0, tc, drain, 0)
    route = route_ref[...]
    f = route[:, 2:3] * buf1[...] + route[:, 3:4] * buf2[...]
    y = _layer_norm_rows(alpha * x_ref[...] + f, g_ref[...], b_ref[...])
    xo_ref[...] = y
    xb_ref[...] = y.astype(BF16)


def _combine_ln(pos1, pos2, y_sorted, x, route, g, b, layer, alpha, tc):
    s, d = x.shape
    row_spec = pl.BlockSpec((tc, d), lambda i, p1, p2: (i, 0))
    par_spec = pl.BlockSpec((None, 1, d), lambda i, p1, p2: (layer, 0, 0))
    return pl.pallas_call(
        functools.partial(_combine_ln_kernel, tc=tc, alpha=alpha),
        grid_spec=pltpu.PrefetchScalarGridSpec(
            num_scalar_prefetch=2,
            grid=(s // tc,),
            in_specs=[pl.BlockSpec(memory_space=pl.ANY), row_spec,
                      pl.BlockSpec((tc, LANES), lambda i, p1, p2: (i, 0)), par_spec, par_spec],
            out_specs=[row_spec, row_spec],
            scratch_shapes=[pltpu.VMEM((tc, d), F32), pltpu.VMEM((tc, d), F32), pltpu.SemaphoreType.DMA],
        ),
        out_shape=[jax.ShapeDtypeStruct((s, d), F32), jax.ShapeDtypeStruct((s, d), BF16)],
        compiler_params=_cparams(("arbitrary",), 32),
        name="moe_combine_ln",
    )(pos1, pos2, y_sorted, x, route, g, b)


def _moe_plan(route, counts, tm, n_tiles):
    expert = route[:, 0:2].astype(jnp.int32)
    rank = route[:, 4:6].astype(jnp.int32)
    count = counts[0, :N_EXPERTS].astype(jnp.int32)
    padded = ((count + tm - 1) // tm) * tm
    ends = jnp.cumsum(padded)
    starts = ends - padded
    pos = starts[expert] + rank
    n_used = ends[-1] // tm
    tile = jnp.arange(n_tiles, dtype=jnp.int32)
    last = jnp.minimum(tile, n_used - 1)
    tile_expert = jnp.sum(last[:, None] * tm >= ends[None, :], axis=1).astype(jnp.int32)
    prev = jnp.concatenate([jnp.full((1,), -1, jnp.int32), tile_expert[:-1]])
    tile_first = (tile_expert != prev).astype(jnp.int32)
    return pos, (tile_expert, tile_first, n_used.reshape(1).astype(jnp.int32))


def _rope_tables(seq):
    pos = jnp.arange(seq, dtype=F32)
    inv = ROPE_THETA ** (-jnp.arange(0, QK_DIM, 2, dtype=F32) / QK_DIM)
    ang = pos[:, None] * inv[None, :]
    cos, sin = jnp.cos(ang), jnp.sin(ang)
    return jnp.tile(cos, (1, 4)), jnp.tile(jnp.concatenate([-sin, sin], axis=1), (1, 2))


def _pad_lanes(a):
    return jnp.pad(a, ((0, 0), (0, LANES - a.shape[1])))


def kernel(x, w_in, b_forget, w_pool_group, pool_scale, diff_lambda, diff_norm_gain, w_branch_a, w_branch_b, w_branch_c, b_gate, w_out, ln1_g, ln1_b, ln2_g, ln2_b, ffn_w_gate, ffn_w_up, ffn_w_down, router_w, router_b, expert_w_gate, expert_w_up, expert_w_down):
    batch, s, d = x.shape
    assert batch == 1 and d == D_MODEL and w_in.shape[-1] == N_MAIN + N_HEADS
    depth = w_in.shape[0]
    alpha = (2 * depth) ** 0.25

    tm = min(1024, s)
    tq = tk = min(512, s)
    te = min(512, s)
    tg = min(256, s)
    n_ff = 512
    n_tiles = 2 * s // te + N_EXPERTS
    n_slots = n_tiles * te

    cos_t, sin_t = _rope_tables(s)
    b_gate_flat = b_gate.reshape(depth, 1, 3 * D_MODEL)
    pool_scale3 = pool_scale.reshape(depth, 1, SEG)
    gain3 = diff_norm_gain.reshape(depth, 1, HEAD_DIM)
    ln = [p.reshape(depth, 1, D_MODEL) for p in (ln1_g, ln1_b, ln2_g, ln2_b)]
    dense_w = [w.reshape(w.shape[0], 1, *w.shape[1:]) for w in (ffn_w_gate, ffn_w_up, ffn_w_down)]

    xf = x.reshape(s, d)
    xb = xf.astype(BF16)
    for layer in range(depth):
        lambda_init = 0.8 - 0.6 * math.exp(-0.3 * layer)
        proj = _inproj(xb, w_in, b_gate_flat, cos_t, sin_t, layer, tm)
        w_fg = _pad_lanes(w_in[layer, :, N_MAIN:])
        b_fg = _pad_lanes(b_forget[layer].reshape(1, N_HEADS))
        q_aug, k_aug = _forget_prep(xb, w_fg, b_fg, proj, tq)
        y_a = _fox_attn(q_aug, k_aug, proj, tq, tk)
        y_b = _pool(proj, w_pool_group, pool_scale3, layer, tm)
        y_c = _diff_attn(proj, diff_lambda, gain3, layer, lambda_init, tq, tk)
        h = _merge(y_a, y_b, y_c, w_branch_a, w_branch_b, w_branch_c, proj, layer, tm, 512)
        mix = _matmul(h, w_out, layer, tm, SEG, F32)
        xf, xb = _add_ln(xf, mix, ln[0], ln[1], layer, alpha, tg)

        j = layer // 2
        if layer % 2 == 0:
            groups = _single_group(s // te)
            hidden = _gate_up(groups, xb, dense_w[0], dense_w[1], j, te, n_ff)
            f = _down(groups, hidden, dense_w[2], j, te, 512)
            xf, xb = _add_ln(xf, f, ln[2], ln[3], layer, alpha, tg)
        else:
            route, counts = _router(xf, _pad_lanes(router_w[j]), _pad_lanes(router_b[j].reshape(1, N_EXPERTS)), te)
            pos, groups = _moe_plan(route, counts, te, n_tiles)
            slot_token = _slot_token(pos.reshape(-1), n_slots)
            x_sorted = _dispatch(slot_token, xf, tg)
            hidden = _gate_up(groups, x_sorted, expert_w_gate, expert_w_up, j, te, n_ff)
            y_sorted = _down(groups, hidden, expert_w_down, j, te, 512)
            xf, xb = _combine_ln(pos[:, 0], pos[:, 1], y_sorted, xf, route, ln[2], ln[3], layer, alpha, tg)
    return xf.reshape(batch, s, d)
```

```python
import functools
import math

import jax
import jax.numpy as jnp
from jax import lax
from jax.experimental import pallas as pl
from jax.experimental.pallas import tpu as pltpu

BF16 = jnp.bfloat16
F32 = jnp.float32

D_MODEL = 2048
N_HEADS = 8
HEAD_DIM = 128
QK_DIM = 64
SEG = 1024
N_MAIN = 13 * SEG
POOL_WINDOWS = (2, 4, 8, 16)
POOL_GROUP_DIM = 256
POOL_HALO = 16
N_EXPERTS = 8
ROPE_THETA = 10000.0
LN_EPS = 1e-5
LOG2E = math.log2(math.e)
FOX_Q_SCALE = HEAD_DIM ** -0.5 * LOG2E
DIFF_Q_SCALE = QK_DIM ** -0.5 * LOG2E
LANES = 128

MIB = 2 ** 20


def _cparams(semantics, vmem_mib):
    return pltpu.CompilerParams(dimension_semantics=semantics, vmem_limit_bytes=vmem_mib * MIB)


def _rope_store(acc, cos, sin, scale, o_ref):
    lane = lax.broadcasted_iota(jnp.int32, (1, LANES), 1)
    first_half = (lane % QK_DIM) < (QK_DIM // 2)
    for c in range(SEG // LANES):
        a = acc[:, c * LANES:(c + 1) * LANES]
        partner = jnp.where(first_half, pltpu.roll(a, LANES - QK_DIM // 2, 1), pltpu.roll(a, QK_DIM // 2, 1))
        o_ref[:, c * LANES:(c + 1) * LANES] = ((a * cos + partner * sin) * scale).astype(BF16)


def _inproj_kernel(x_ref, w_ref, bg_ref, cos_ref, sin_ref, o_ref, wbf_ref):
    j = pl.program_id(0)
    i = pl.program_id(1)

    @pl.when(i == 0)
    def _cast_weights():
        wbf_ref[...] = w_ref[...].T.astype(BF16)

    acc = jnp.dot(x_ref[...], wbf_ref[...], preferred_element_type=F32)

    @pl.when(j == 0)
    def _fox_q():
        o_ref[...] = (acc * FOX_Q_SCALE).astype(BF16)

    @pl.when((j == 1) | (j == 2) | (j == 5) | (j == 6))
    def _plain():
        o_ref[...] = acc.astype(BF16)

    @pl.when(j == 3)
    def _diff_q():
        _rope_store(acc, cos_ref[...], sin_ref[...], DIFF_Q_SCALE, o_ref)

    @pl.when(j == 4)
    def _diff_k():
        _rope_store(acc, cos_ref[...], sin_ref[...], 1.0, o_ref)

    @pl.when(j >= 7)
    def _gates():
        o_ref[...] = jax.nn.sigmoid(acc + bg_ref[...]).astype(BF16)


def _inproj(x_bf, w_in_t, b_gate_flat, cos_t, sin_t, layer, tm):
    s, d = x_bf.shape
    return pl.pallas_call(
        _inproj_kernel,
        grid=(N_MAIN // SEG, s // tm),
        in_specs=[
            pl.BlockSpec((tm, d), lambda j, i: (i, 0)),
            pl.BlockSpec((None, SEG, d), lambda j, i: (layer, j, 0)),
            pl.BlockSpec((None, 1, SEG), lambda j, i: (layer, 0, jnp.maximum(j - 7, 0))),
            pl.BlockSpec((tm, LANES), lambda j, i: (i, 0)),
            pl.BlockSpec((tm, LANES), lambda j, i: (i, 0)),
        ],
        out_specs=pl.BlockSpec((tm, SEG), lambda j, i: (i, j)),
        out_shape=jax.ShapeDtypeStruct((s, N_MAIN), BF16),
        scratch_shapes=[pltpu.VMEM((d, SEG), BF16)],
        compiler_params=_cparams(("arbitrary", "arbitrary"), 48),
        name="inproj",
    )(x_bf, w_in_t, b_gate_flat, cos_t, sin_t)


def _split3(v):
    hi = v.astype(BF16)
    r = v - hi.astype(F32)
    mid = r.astype(BF16)
    lo = (r - mid.astype(F32)).astype(BF16)
    return hi, mid, lo


def _forget_prep_kernel(x_ref, wfg_ref, bf_ref, fq_ref, fk_ref, qa_ref, ka_ref, carry_ref, *, tm):
    i = pl.program_id(0)

    @pl.when(i == 0)
    def _init():
        carry_ref[...] = jnp.zeros_like(carry_ref)

    w_fg = jnp.concatenate([wfg_ref[...], jnp.zeros((LANES - N_HEADS, wfg_ref.shape[1]), F32)], axis=0)
    z = lax.dot_general(x_ref[...], w_fg.astype(BF16), (((1,), (1,)), ((), ())),
                        preferred_element_type=F32) + bf_ref[...]
    log_f = -(jnp.maximum(-z, 0.0) + jnp.log1p(jnp.exp(-jnp.abs(z)))) * LOG2E
    row = lax.broadcasted_iota(jnp.int32, (tm, tm), 0)
    col = lax.broadcasted_iota(jnp.int32, (tm, tm), 1)
    tri = (col <= row).astype(BF16)
    hi, mid, lo = _split3(log_f)
    cum = (jnp.dot(tri, hi, preferred_element_type=F32) + jnp.dot(tri, mid, preferred_element_type=F32)
           + jnp.dot(tri, lo, preferred_element_type=F32)) + carry_ref[...]
    carry_ref[...] = cum[tm - 1:tm, :]
    c_hi, c_mid, c_lo = (p.astype(F32) for p in _split3(cum))
    lane = lax.broadcasted_iota(jnp.int32, (1, LANES), 1)
    for h in range(N_HEADS):
        ch, cm, cl = c_hi[:, h:h + 1], c_mid[:, h:h + 1], c_lo[:, h:h + 1]
        q_extra = jnp.where(lane == 0, ch, jnp.where(lane == 1, cm, jnp.where(lane == 2, cl,
                  jnp.where(lane < 6, 1.0, 0.0))))
        k_extra = jnp.where(lane < 3, 1.0, jnp.where(lane == 3, -ch, jnp.where(lane == 4, -cm,
                  jnp.where(lane == 5, -cl, 0.0))))
        q_aug = jnp.concatenate([fq_ref[:, h * HEAD_DIM:(h + 1) * HEAD_DIM].astype(F32), q_extra], axis=1)
        qa_ref[h] = q_aug.T.astype(BF16)
        ka_ref[h, :, 0:HEAD_DIM] = fk_ref[:, h * HEAD_DIM:(h + 1) * HEAD_DIM]
        ka_ref[h, :, HEAD_DIM:2 * HEAD_DIM] = k_extra.astype(BF16)


def _forget_prep(x_bf, w_in_t, b_fg, proj, layer, tm):
    s, d = x_bf.shape
    return pl.pallas_call(
        functools.partial(_forget_prep_kernel, tm=tm),
        grid=(s // tm,),
        in_specs=[
            pl.BlockSpec((tm, d), lambda i: (i, 0)),
            pl.BlockSpec((None, N_HEADS, d), lambda i: (layer, N_MAIN // N_HEADS, 0)),
            pl.BlockSpec((1, LANES), lambda i: (0, 0)),
            pl.BlockSpec((tm, SEG), lambda i: (i, 0)),
            pl.BlockSpec((tm, SEG), lambda i: (i, 1)),
        ],
        out_specs=[pl.BlockSpec((N_HEADS, None, 2 * HEAD_DIM, tm), lambda i: (0, i, 0, 0)),
                   pl.BlockSpec((N_HEADS, tm, 2 * HEAD_DIM), lambda i: (0, i, 0))],
        out_shape=[jax.ShapeDtypeStruct((N_HEADS, s // tm, 2 * HEAD_DIM, tm), BF16),
                   jax.ShapeDtypeStruct((N_HEADS, s, 2 * HEAD_DIM), BF16)],
        scratch_shapes=[pltpu.VMEM((1, LANES), F32)],
        compiler_params=_cparams(("arbitrary",), 32),
        name="forget_prep",
    )(x_bf, w_in_t, b_fg, proj, proj)


TRANSPOSED_SEGS = (2, 3, 5)


def _transpose_kernel(x_ref, o_ref, *, tk):
    xt = x_ref[...].astype(F32).T
    for c in range(o_ref.shape[0]):
        o_ref[c] = xt[:, c * tk:(c + 1) * tk].astype(BF16)


def _transpose_segs(proj, tm, tk):
    s = proj.shape[0]
    seg_of = lambda g: jnp.where(g == 2, TRANSPOSED_SEGS[2], g + TRANSPOSED_SEGS[0])
    return pl.pallas_call(
        functools.partial(_transpose_kernel, tk=tk),
        grid=(len(TRANSPOSED_SEGS), s // tm),
        in_specs=[pl.BlockSpec((tm, SEG), lambda g, i: (i, seg_of(g)))],
        out_specs=pl.BlockSpec((None, tm // tk, SEG, tk), lambda g, i: (g, i, 0, 0)),
        out_shape=jax.ShapeDtypeStruct((len(TRANSPOSED_SEGS), s // tk, SEG, tk), BF16),
        compiler_params=_cparams(("arbitrary", "arbitrary"), 32),
        name="transpose_segs",
    )(proj)


class _AttnState:
    def __init__(self, refs):
        self.s = refs[0:2]
        self.p = refs[2:4]
        self.alpha = refs[4:6]
        self.m, self.l, self.acc = refs[6:9]

    N_REFS = 9

    @staticmethod
    def scratch(tq, tk):
        row = pltpu.VMEM((1, tq), F32)
        return ([pltpu.VMEM((tk, tq), F32)] * 2 + [pltpu.VMEM((tk, tq), BF16)] * 2 + [row] * 2
                + [row, row, pltpu.VMEM((HEAD_DIM, tq), F32)])

    def init(self):
        self.m[...] = jnp.full_like(self.m, -jnp.inf)
        self.l[...] = jnp.zeros_like(self.l)
        self.acc[...] = jnp.zeros_like(self.acc)
        self.p[1][...] = jnp.zeros_like(self.p[1])
        self.alpha[1][...] = jnp.ones_like(self.alpha[1])


def _attn_pipeline(i, q_t, k_ref, v_t_ref, states, tq, tk):
    assert tq == 2 * tk

    def scores(c, slot):
        k = k_ref[pl.ds(pl.multiple_of(c * tk, tk), tk), :]
        for st, q in zip(states, q_t):
            st.s[slot][...] = jnp.dot(k, q, preferred_element_type=F32)

    def softmax(slot, k0=None):
        for st in states:
            s = st.s[slot][...]
            if k0 is not None:
                key = lax.broadcasted_iota(jnp.int32, s.shape, 0) + k0
                query = lax.broadcasted_iota(jnp.int32, s.shape, 1) + i * tq
                s = jnp.where(key <= query, s, -jnp.inf)
            m_prev = st.m[...]
            m_new = jnp.maximum(m_prev, jnp.max(s, axis=0, keepdims=True))
            alpha = jnp.exp2(m_prev - m_new)
            p = jnp.exp2(s - m_new)
            st.l[...] = alpha * st.l[...] + jnp.sum(p, axis=0, keepdims=True)
            st.m[...] = m_new
            st.alpha[slot][...] = alpha
            st.p[slot][...] = p.astype(BF16)

    def values(c, slot):
        v_t = v_t_ref[c]
        for st in states:
            st.acc[...] = st.alpha[slot][...] * st.acc[...] + jnp.dot(v_t, st.p[slot][...],
                                                                       preferred_element_type=F32)

    for st in states:
        st.init()
    scores(0, 0)

    def body(pair, carry):
        c = 2 * pair
        scores(c + 1, 1)
        softmax(0)
        values(jnp.maximum(c - 1, 0), 1)
        scores(c + 2, 0)
        softmax(1)
        values(c, 0)
        return carry

    lax.fori_loop(0, i, body, 0)
    c = 2 * i
    scores(c + 1, 1)
    softmax(0, k0=c * tk)
    values(jnp.maximum(c - 1, 0), 1)
    softmax(1, k0=(c + 1) * tk)
    values(c, 0)
    values(c + 1, 1)


def _fox_attn_kernel(q_ref, k_ref, v_ref, o_ref, *scratch, tq, tk):
    st = _AttnState(scratch)
    _attn_pipeline(pl.program_id(1), [q_ref[...]], k_ref, v_ref, [st], tq, tk)
    o_ref[...] = (st.acc[...] / st.l[...]).T.astype(BF16)


def _fox_attn(q_aug_t, k_aug, segs_t, tq, tk):
    _, s, _ = k_aug.shape
    return pl.pallas_call(
        functools.partial(_fox_attn_kernel, tq=tq, tk=tk),
        grid=(N_HEADS, s // tq),
        in_specs=[
            pl.BlockSpec((None, None, 2 * HEAD_DIM, tq), lambda h, i: (h, i, 0, 0)),
            pl.BlockSpec((None, s, 2 * HEAD_DIM), lambda h, i: (h, 0, 0)),
            pl.BlockSpec((None, s // tk, HEAD_DIM, tk), lambda h, i: (0, 0, h, 0)),
        ],
        out_specs=pl.BlockSpec((tq, HEAD_DIM), lambda h, i: (i, h)),
        out_shape=jax.ShapeDtypeStruct((s, SEG), BF16),
        scratch_shapes=_AttnState.scratch(tq, tk),
        compiler_params=_cparams(("arbitrary", "arbitrary"), 40),
        name="fox_attn",
    )(q_aug_t, k_aug, segs_t)


def _diff_attn_kernel(q_ref, k_ref, v_ref, dl_ref, gain_ref, o_ref, *scratch, tq, tk, lambda_init):
    st1 = _AttnState(scratch[:_AttnState.N_REFS])
    st2 = _AttnState(scratch[_AttnState.N_REFS:])
    q_t = jnp.concatenate([q_ref[0], q_ref[1]], axis=1)
    channel = lax.broadcasted_iota(jnp.int32, (HEAD_DIM, 1), 0)
    q1 = jnp.where(channel < QK_DIM, q_t, jnp.zeros_like(q_t))
    q2 = jnp.where(channel >= QK_DIM, q_t, jnp.zeros_like(q_t))
    _attn_pipeline(pl.program_id(1), [q1, q2], k_ref, v_ref, [st1, st2], tq, tk)

    dl = dl_ref[...]
    lam = (jnp.exp(jnp.sum(dl[0:1, :] * dl[1:2, :], axis=1, keepdims=True))
           - jnp.exp(jnp.sum(dl[2:3, :] * dl[3:4, :], axis=1, keepdims=True)) + lambda_init)
    o = st1.acc[...] / st1.l[...] - lam * (st2.acc[...] / st2.l[...])
    o = o * lax.rsqrt(jnp.mean(o * o, axis=0, keepdims=True) + LN_EPS)
    o_ref[...] = (o.T * gain_ref[...] * (1.0 - lambda_init)).astype(BF16)


def _diff_attn(proj, segs_t, diff_lambda, diff_gain, layer, lambda_init, tq, tk):
    s = proj.shape[0]
    k_block0 = 4 * SEG // HEAD_DIM
    return pl.pallas_call(
        functools.partial(_diff_attn_kernel, tq=tq, tk=tk, lambda_init=lambda_init),
        grid=(N_HEADS, s // tq),
        in_specs=[
            pl.BlockSpec((None, tq // tk, HEAD_DIM, tk), lambda h, i: (1, i, h, 0)),
            pl.BlockSpec((s, HEAD_DIM), lambda h, i: (0, k_block0 + h)),
            pl.BlockSpec((None, s // tk, HEAD_DIM, tk), lambda h, i: (2, 0, h, 0)),
            pl.BlockSpec((None, 4, QK_DIM), lambda h, i: (layer, 0, 0)),
            pl.BlockSpec((None, 1, HEAD_DIM), lambda h, i: (layer, 0, 0)),
        ],
        out_specs=pl.BlockSpec((tq, HEAD_DIM), lambda h, i: (i, h)),
        out_shape=jax.ShapeDtypeStruct((s, SEG), BF16),
        scratch_shapes=_AttnState.scratch(tq, tk) * 2,
        compiler_params=_cparams(("arbitrary", "arbitrary"), 40),
        name="diff_attn",
    )(segs_t, proj, segs_t, diff_lambda, diff_gain)


def _pool_kernel(u_ref, halo_ref, wg_ref, sc_ref, o_ref, *, tm):
    i = pl.program_id(0)
    tokens_seen = (lax.broadcasted_iota(jnp.int32, (tm, 1), 0) + i * tm + 1).astype(F32)
    for g, window in enumerate(POOL_WINDOWS):
        cols = slice(g * POOL_GROUP_DIM, (g + 1) * POOL_GROUP_DIM)
        u = u_ref[:, cols].astype(F32)
        halo = halo_ref[:, cols].astype(F32)
        halo = jnp.where(i > 0, halo, jnp.zeros_like(halo))
        ext = jnp.concatenate([halo, u], axis=0)
        shift = 1
        while shift < window:
            ext = ext + pltpu.roll(ext, shift, 0)
            shift *= 2
        pooled = ext[POOL_HALO:, :] / jnp.minimum(tokens_seen, float(window))
        delta = (pooled - u).astype(BF16)
        y = jnp.dot(delta, wg_ref[g].astype(BF16), preferred_element_type=F32) * sc_ref[:, cols]
        o_ref[:, cols] = y.astype(BF16)


def _pool(proj, w_pool_group, pool_scale, layer, tm):
    s = proj.shape[0]
    n_groups = len(POOL_WINDOWS)
    halo_blocks_per_tile = tm // POOL_HALO
    return pl.pallas_call(
        functools.partial(_pool_kernel, tm=tm),
        grid=(s // tm,),
        in_specs=[
            pl.BlockSpec((tm, SEG), lambda i: (i, 6)),
            pl.BlockSpec((POOL_HALO, SEG), lambda i: (jnp.maximum(i * halo_blocks_per_tile - 1, 0), 6)),
            pl.BlockSpec((None, n_groups, POOL_GROUP_DIM, POOL_GROUP_DIM), lambda i: (layer, 0, 0, 0)),
            pl.BlockSpec((None, 1, SEG), lambda i: (layer, 0, 0)),
        ],
        out_specs=pl.BlockSpec((tm, SEG), lambda i: (i, 0)),
        out_shape=jax.ShapeDtypeStruct((s, SEG), BF16),
        compiler_params=_cparams(("arbitrary",), 32),
        name="pool",
    )(proj, proj, w_pool_group, pool_scale)


def _merge_kernel(ya_ref, yb_ref, yc_ref, wa_ref, wb_ref, wc_ref, ga_ref, gb_ref, gc_ref, o_ref,
                  wab_ref, wbb_ref, wcb_ref):
    i = pl.program_id(1)

    @pl.when(i == 0)
    def _cast_weights():
        wab_ref[...] = wa_ref[...].astype(BF16)
        wbb_ref[...] = wb_ref[...].astype(BF16)
        wcb_ref[...] = wc_ref[...].astype(BF16)

    h = ga_ref[...].astype(F32) * jnp.dot(ya_ref[...], wab_ref[...], preferred_element_type=F32)
    h = h + gb_ref[...].astype(F32) * jnp.dot(yb_ref[...], wbb_ref[...], preferred_element_type=F32)
    h = h + gc_ref[...].astype(F32) * jnp.dot(yc_ref[...], wcb_ref[...], preferred_element_type=F32)
    o_ref[...] = h.astype(BF16)


def _merge(y_a, y_b, y_c, w_a, w_b, w_c, proj, layer, tm, tn):
    s = y_a.shape[0]
    gate_block0 = 7 * SEG // tn
    gate_blocks = D_MODEL // tn
    y_spec = pl.BlockSpec((tm, SEG), lambda j, i: (i, 0))
    w_spec = pl.BlockSpec((None, SEG, tn), lambda j, i: (layer, 0, j))
    gate_spec = lambda br: pl.BlockSpec((tm, tn), lambda j, i: (i, gate_block0 + br * gate_blocks + j))
    return pl.pallas_call(
        _merge_kernel,
        grid=(D_MODEL // tn, s // tm),
        in_specs=[y_spec, y_spec, y_spec, w_spec, w_spec, w_spec, gate_spec(0), gate_spec(1), gate_spec(2)],
        out_specs=pl.BlockSpec((tm, tn), lambda j, i: (i, j)),
        out_shape=jax.ShapeDtypeStruct((s, D_MODEL), BF16),
        scratch_shapes=[pltpu.VMEM((SEG, tn), BF16)] * 3,
        compiler_params=_cparams(("arbitrary", "arbitrary"), 48),
        name="merge",
    )(y_a, y_b, y_c, w_a, w_b, w_c, proj, proj, proj)


def _matmul_kernel(x_ref, w_ref, o_ref, wbf_ref):
    @pl.when(pl.program_id(1) == 0)
    def _cast_weights():
        wbf_ref[...] = w_ref[...].astype(BF16)

    o_ref[...] = jnp.dot(x_ref[...], wbf_ref[...], preferred_element_type=F32).astype(o_ref.dtype)


def _matmul(x_bf, w, layer, tm, tn, out_dtype):
    s, k = x_bf.shape
    n = w.shape[-1]
    return pl.pallas_call(
        _matmul_kernel,
        grid=(n // tn, s // tm),
        in_specs=[pl.BlockSpec((tm, k), lambda j, i: (i, 0)),
                  pl.BlockSpec((None, k, tn), lambda j, i: (layer, 0, j))],
        out_specs=pl.BlockSpec((tm, tn), lambda j, i: (i, j)),
        out_shape=jax.ShapeDtypeStruct((s, n), out_dtype),
        scratch_shapes=[pltpu.VMEM((k, tn), BF16)],
        compiler_params=_cparams(("arbitrary", "arbitrary"), 48),
        name="matmul",
    )(x_bf, w)


def _layer_norm_rows(z, g, b):
    mu = jnp.mean(z, axis=1, keepdims=True)
    zc = z - mu
    var = jnp.mean(zc * zc, axis=1, keepdims=True)
    return zc * lax.rsqrt(var + LN_EPS) * g + b


def _add_ln_kernel(x_ref, f_ref, g_ref, b_ref, xo_ref, xb_ref, *, alpha):
    y = _layer_norm_rows(alpha * x_ref[...] + f_ref[...], g_ref[...], b_ref[...])
    xo_ref[...] = y
    xb_ref[...] = y.astype(BF16)


def _add_ln(x, f, g, b, layer, alpha, tm):
    s, d = x.shape
    row_spec = pl.BlockSpec((tm, d), lambda i: (i, 0))
    par_spec = pl.BlockSpec((None, 1, d), lambda i: (layer, 0, 0))
    return pl.pallas_call(
        functools.partial(_add_ln_kernel, alpha=alpha),
        grid=(s // tm,),
        in_specs=[row_spec, row_spec, par_spec, par_spec],
        out_specs=[row_spec, row_spec],
        out_shape=[jax.ShapeDtypeStruct((s, d), F32), jax.ShapeDtypeStruct((s, d), BF16)],
        compiler_params=_cparams(("arbitrary",), 32),
        name="add_ln",
    )(x, f, g, b)


def _gate_up_kernel(te_ref, tf_ref, nu_ref, x_ref, wg_ref, wu_ref, o_ref, wgb_ref, wub_ref):
    t = pl.program_id(1)

    @pl.when(tf_ref[t] == 1)
    def _cast_weights():
        wgb_ref[...] = wg_ref[...].astype(BF16)
        wub_ref[...] = wu_ref[...].astype(BF16)

    @pl.when(t < nu_ref[0])
    def _compute():
        x = x_ref[...]
        g = jnp.dot(x, wgb_ref[...], preferred_element_type=F32)
        u = jnp.dot(x, wub_ref[...], preferred_element_type=F32)
        o_ref[...] = (g * jax.nn.sigmoid(g) * u).astype(BF16)

    @pl.when(t >= nu_ref[0])
    def _unused_tile():
        o_ref[...] = jnp.zeros_like(o_ref)


def _gate_up(groups, x_sorted, w_gate, w_up, layer, tm, tn):
    tile_expert, tile_first, n_used = groups
    n_slots, k = x_sorted.shape
    n = w_gate.shape[-1]
    row = lambda j, t, te, tf, nu: (jnp.minimum(t, nu[0] - 1), 0)
    w_spec = pl.BlockSpec((None, None, k, tn), lambda j, t, te, tf, nu: (layer, te[t], 0, j))
    return pl.pallas_call(
        _gate_up_kernel,
        grid_spec=pltpu.PrefetchScalarGridSpec(
            num_scalar_prefetch=3,
            grid=(n // tn, n_slots // tm),
            in_specs=[pl.BlockSpec((tm, k), row), w_spec, w_spec],
            out_specs=pl.BlockSpec((tm, tn), lambda j, t, te, tf, nu: (t, j)),
            scratch_shapes=[pltpu.VMEM((k, tn), BF16)] * 2,
        ),
        out_shape=jax.ShapeDtypeStruct((n_slots, n), BF16),
        compiler_params=_cparams(("arbitrary", "arbitrary"), 48),
        name="ffn_gate_up",
    )(tile_expert, tile_first, n_used, x_sorted, w_gate, w_up)


def _down_kernel(te_ref, tf_ref, nu_ref, h_ref, w_ref, o_ref, wbf_ref):
    t = pl.program_id(1)

    @pl.when(tf_ref[t] == 1)
    def _cast_weights():
        wbf_ref[...] = w_ref[...].astype(BF16)

    @pl.when(t < nu_ref[0])
    def _compute():
        o_ref[...] = jnp.dot(h_ref[...], wbf_ref[...], preferred_element_type=F32)

    @pl.when(t >= nu_ref[0])
    def _unused_tile():
        o_ref[...] = jnp.zeros_like(o_ref)


def _down(groups, h_sorted, w_down, layer, tm, tn):
    tile_expert, tile_first, n_used = groups
    n_slots, k = h_sorted.shape
    n = w_down.shape[-1]
    return pl.pallas_call(
        _down_kernel,
        grid_spec=pltpu.PrefetchScalarGridSpec(
            num_scalar_prefetch=3,
            grid=(n // tn, n_slots // tm),
            in_specs=[pl.BlockSpec((tm, k), lambda j, t, te, tf, nu: (jnp.minimum(t, nu[0] - 1), 0)),
                      pl.BlockSpec((None, None, k, tn), lambda j, t, te, tf, nu: (layer, te[t], 0, j))],
            out_specs=pl.BlockSpec((tm, tn), lambda j, t, te, tf, nu: (t, j)),
            scratch_shapes=[pltpu.VMEM((k, tn), BF16)],
        ),
        out_shape=jax.ShapeDtypeStruct((n_slots, n), F32),
        compiler_params=_cparams(("arbitrary", "arbitrary"), 52),
        name="ffn_down",
    )(tile_expert, tile_first, n_used, h_sorted, w_down)


def _single_group(n_tiles):
    return (jnp.zeros((n_tiles,), jnp.int32),
            jnp.zeros((n_tiles,), jnp.int32).at[0].set(1),
            jnp.full((1,), n_tiles, jnp.int32))


def _router_kernel(x_ref, rw_ref, rb_ref, o_ref, cnt_ref, carry_ref, *, tm):
    i = pl.program_id(0)

    @pl.when(i == 0)
    def _init():
        carry_ref[...] = jnp.zeros_like(carry_ref)

    logits = jnp.dot(x_ref[...], rw_ref[...], preferred_element_type=F32,
                     precision=lax.Precision.HIGHEST) + rb_ref[...]
    lane = lax.broadcasted_iota(jnp.int32, (tm, LANES), 1).astype(F32)
    lg = jnp.where(lane < N_EXPERTS, logits, -jnp.inf)
    v1 = jnp.max(lg, axis=1, keepdims=True)
    e1 = jnp.min(jnp.where(lg == v1, lane, float(LANES)), axis=1, keepdims=True)
    lg2 = jnp.where(lane == e1, -jnp.inf, lg)
    v2 = jnp.max(lg2, axis=1, keepdims=True)
    e2 = jnp.min(jnp.where(lg2 == v2, lane, float(LANES)), axis=1, keepdims=True)
    t2 = jnp.exp(v2 - v1)
    w1 = 1.0 / (1.0 + t2)
    w2 = t2 / (1.0 + t2)
    pick1 = lane == e1
    pick2 = lane == e2
    onehot = jnp.where(pick1 | pick2, 1.0, 0.0)
    row = lax.broadcasted_iota(jnp.int32, (tm, tm), 0)
    col = lax.broadcasted_iota(jnp.int32, (tm, tm), 1)
    before = (col < row).astype(BF16)
    seen = jnp.dot(before, onehot.astype(BF16), preferred_element_type=F32) + carry_ref[...]
    r1 = jnp.sum(jnp.where(pick1, seen, 0.0), axis=1, keepdims=True)
    r2 = jnp.sum(jnp.where(pick2, seen, 0.0), axis=1, keepdims=True)
    carry_ref[...] = carry_ref[...] + jnp.sum(onehot, axis=0, keepdims=True)
    o_ref[...] = jnp.where(lane == 0, e1, jnp.where(lane == 1, e2, jnp.where(lane == 2, w1,
                 jnp.where(lane == 3, w2, jnp.where(lane == 4, r1, jnp.where(lane == 5, r2, 0.0))))))
    cnt_ref[...] = carry_ref[...]


def _router(x, rw_pad, rb_pad, tm):
    s, d = x.shape
    return pl.pallas_call(
        functools.partial(_router_kernel, tm=tm),
        grid=(s // tm,),
        in_specs=[pl.BlockSpec((tm, d), lambda i: (i, 0)),
                  pl.BlockSpec((d, LANES), lambda i: (0, 0)),
                  pl.BlockSpec((1, LANES), lambda i: (0, 0))],
        out_specs=[pl.BlockSpec((tm, LANES), lambda i: (i, 0)), pl.BlockSpec((1, LANES), lambda i: (0, 0))],
        out_shape=[jax.ShapeDtypeStruct((s, LANES), F32), jax.ShapeDtypeStruct((1, LANES), F32)],
        scratch_shapes=[pltpu.VMEM((1, LANES), F32)],
        compiler_params=_cparams(("arbitrary",), 32),
        name="router",
    )(x, rw_pad, rb_pad)


def _slot_token_kernel(pos_ref, tok_ref, *, n_assign, n_slots):
    def clear(sl, carry):
        tok_ref[sl] = 0
        return carry

    lax.fori_loop(0, n_slots, clear, 0)

    def place(a, carry):
        tok_ref[pos_ref[a]] = lax.shift_right_logical(a, 1)
        return carry

    lax.fori_loop(0, n_assign, place, 0)


def _slot_token(pos_flat, n_slots):
    n_assign = pos_flat.shape[0]
    return pl.pallas_call(
        functools.partial(_slot_token_kernel, n_assign=n_assign, n_slots=n_slots),
        in_specs=[pl.BlockSpec(memory_space=pltpu.SMEM)],
        out_specs=pl.BlockSpec(memory_space=pltpu.SMEM),
        out_shape=jax.ShapeDtypeStruct((n_slots,), jnp.int32),
        name="slot_token",
    )(pos_flat)


def _row_copy(src_hbm, src_row, buf, dst_row, sem):
    return pltpu.make_async_copy(src_hbm.at[pl.ds(src_row, 1), :], buf.at[pl.ds(dst_row, 1), :], sem)


def _dispatch_kernel(tok_ref, x_hbm, o_ref, buf, sem, *, tg):
    base = pl.program_id(0) * tg

    def issue(r, carry):
        _row_copy(x_hbm, tok_ref[base + r], buf, r, sem).start()
        return carry

    lax.fori_loop(0, tg, issue, 0)

    def drain(r, carry):
        _row_copy(x_hbm, 0, buf, r, sem).wait()
        return carry

    lax.fori_loop(0, tg, drain, 0)
    o_ref[...] = buf[...].astype(BF16)


def _dispatch(slot_token, x, tg):
    s, d = x.shape
    n_slots = slot_token.shape[0]
    return pl.pallas_call(
        functools.partial(_dispatch_kernel, tg=tg),
        grid_spec=pltpu.PrefetchScalarGridSpec(
            num_scalar_prefetch=1,
            grid=(n_slots // tg,),
            in_specs=[pl.BlockSpec(memory_space=pl.ANY)],
            out_specs=pl.BlockSpec((tg, d), lambda i, tok: (i, 0)),
            scratch_shapes=[pltpu.VMEM((tg, d), F32), pltpu.SemaphoreType.DMA],
        ),
        out_shape=jax.ShapeDtypeStruct((n_slots, d), BF16),
        compiler_params=_cparams(("arbitrary",), 32),
        name="moe_dispatch",
    )(slot_token, x)


def _combine_ln_kernel(p1_ref, p2_ref, y_hbm, x_ref, route_ref, g_ref, b_ref, xo_ref, xb_ref,
                       buf1, buf2, sem, *, tc, alpha):
    base = pl.program_id(0) * tc

    def issue(r, carry):
        _row_copy(y_hbm, p1_ref[base + r], buf1, r, sem).start()
        _row_copy(y_hbm, p2_ref[base + r], buf2, r, sem).start()
        return carry

    lax.fori_loop(0, tc, issue, 0)

    def drain(r, carry):
        _row_copy(y_hbm, 0, buf1, r, sem).wait()
        _row_copy(y_hbm, 0, buf2, r, sem).wait()
        return carry

    lax.fori_loop(0, tc, drain, 0)
    route = route_ref[...]
    f = route[:, 2:3] * buf1[...] + route[:, 3:4] * buf2[...]
    y = _layer_norm_rows(alpha * x_ref[...] + f, g_ref[...], b_ref[...])
    xo_ref[...] = y
    xb_ref[...] = y.astype(BF16)


def _combine_ln(pos1, pos2, y_sorted, x, route, g, b, layer, alpha, tc):
    s, d = x.shape
    row_spec = pl.BlockSpec((tc, d), lambda i, p1, p2: (i, 0))
    par_spec = pl.BlockSpec((None, 1, d), lambda i, p1, p2: (layer, 0, 0))
    return pl.pallas_call(
        functools.partial(_combine_ln_kernel, tc=tc, alpha=alpha),
        grid_spec=pltpu.PrefetchScalarGridSpec(
            num_scalar_prefetch=2,
            grid=(s // tc,),
            in_specs=[pl.BlockSpec(memory_space=pl.ANY), row_spec,
                      pl.BlockSpec((tc, LANES), lambda i, p1, p2: (i, 0)), par_spec, par_spec],
            out_specs=[row_spec, row_spec],
            scratch_shapes=[pltpu.VMEM((tc, d), F32), pltpu.VMEM((tc, d), F32), pltpu.SemaphoreType.DMA],
        ),
        out_shape=[jax.ShapeDtypeStruct((s, d), F32), jax.ShapeDtypeStruct((s, d), BF16)],
        compiler_params=_cparams(("arbitrary",), 32),
        name="moe_combine_ln",
    )(pos1, pos2, y_sorted, x, route, g, b)


def _moe_plan(route, counts, tm, n_tiles):
    expert = route[:, 0:2].astype(jnp.int32)
    rank = route[:, 4:6].astype(jnp.int32)
    count = counts[0, :N_EXPERTS].astype(jnp.int32)
    padded = ((count + tm - 1) // tm) * tm
    ends = jnp.cumsum(padded)
    starts = ends - padded
    pos = starts[expert] + rank
    n_used = ends[-1] // tm
    tile = jnp.arange(n_tiles, dtype=jnp.int32)
    last = jnp.minimum(tile, n_used - 1)
    tile_expert = jnp.sum(last[:, None] * tm >= ends[None, :], axis=1).astype(jnp.int32)
    prev = jnp.concatenate([jnp.full((1,), -1, jnp.int32), tile_expert[:-1]])
    tile_first = (tile_expert != prev).astype(jnp.int32)
    return pos, (tile_expert, tile_first, n_used.reshape(1).astype(jnp.int32))


def _rope_tables(seq):
    pos = jnp.arange(seq, dtype=F32)
    inv = ROPE_THETA ** (-jnp.arange(0, QK_DIM, 2, dtype=F32) / QK_DIM)
    ang = pos[:, None] * inv[None, :]
    cos, sin = jnp.cos(ang), jnp.sin(ang)
    return jnp.tile(cos, (1, 4)), jnp.tile(jnp.concatenate([-sin, sin], axis=1), (1, 2))


def _pad_lanes(a):
    return jnp.pad(a, ((0, 0), (0, LANES - a.shape[1])))


def kernel(x, w_in, b_forget, w_pool_group, pool_scale, diff_lambda, diff_norm_gain, w_branch_a, w_branch_b, w_branch_c, b_gate, w_out, ln1_g, ln1_b, ln2_g, ln2_b, ffn_w_gate, ffn_w_up, ffn_w_down, router_w, router_b, expert_w_gate, expert_w_up, expert_w_down):
    batch, s, d = x.shape
    assert batch == 1 and d == D_MODEL and w_in.shape[-1] == N_MAIN + N_HEADS
    depth = w_in.shape[0]
    alpha = (2 * depth) ** 0.25

    tm = min(1024, s)
    tq = min(512, s)
    tk = tq // 2
    te = min(512, s)
    tg = min(256, s)
    n_ff = 512
    n_tiles = 2 * s // te + N_EXPERTS
    n_slots = n_tiles * te

    cos_t, sin_t = _rope_tables(s)
    w_in_t = jnp.swapaxes(w_in, 1, 2)
    b_gate_flat = b_gate.reshape(depth, 1, 3 * D_MODEL)
    pool_scale3 = pool_scale.reshape(depth, 1, SEG)
    gain3 = diff_norm_gain.reshape(depth, 1, HEAD_DIM)
    ln = [p.reshape(depth, 1, D_MODEL) for p in (ln1_g, ln1_b, ln2_g, ln2_b)]
    dense_w = [w.reshape(w.shape[0], 1, *w.shape[1:]) for w in (ffn_w_gate, ffn_w_up, ffn_w_down)]

    xf = x.reshape(s, d)
    xb = xf.astype(BF16)
    for layer in range(depth):
        lambda_init = 0.8 - 0.6 * math.exp(-0.3 * layer)
        proj = _inproj(xb, w_in_t, b_gate_flat, cos_t, sin_t, layer, tm)
        b_fg = _pad_lanes(b_forget[layer].reshape(1, N_HEADS))
        q_aug_t, k_aug = _forget_prep(xb, w_in_t, b_fg, proj, layer, tq)
        segs_t = _transpose_segs(proj, tq, tk)
        y_a = _fox_attn(q_aug_t, k_aug, segs_t, tq, tk)
        y_b = _pool(proj, w_pool_group, pool_scale3, layer, tm)
        y_c = _diff_attn(proj, segs_t, diff_lambda, gain3, layer, lambda_init, tq, tk)
        h = _merge(y_a, y_b, y_c, w_branch_a, w_branch_b, w_branch_c, proj, layer, tm, 512)
        mix = _matmul(h, w_out, layer, tm, SEG, F32)
        xf, xb = _add_ln(xf, mix, ln[0], ln[1], layer, alpha, tg)

        j = layer // 2
        if layer % 2 == 0:
            groups = _single_group(s // te)
            hidden = _gate_up(groups, xb, dense_w[0], dense_w[1], j, te, n_ff)
            f = _down(groups, hidden, dense_w[2], j, te, 512)
            xf, xb = _add_ln(xf, f, ln[2], ln[3], layer, alpha, tg)
        else:
            route, counts = _router(xf, _pad_lanes(router_w[j]), _pad_lanes(router_b[j].reshape(1, N_EXPERTS)), te)
            pos, groups = _moe_plan(route, counts, te, n_tiles)
            slot_token = _slot_token(pos.reshape(-1), n_slots)
            x_sorted = _dispatch(slot_token, xf, tg)
            hidden = _gate_up(groups, x_sorted, expert_w_gate, expert_w_up, j, te, n_ff)
            y_sorted = _down(groups, hidden, expert_w_down, j, te, 512)
            xf, xb = _combine_ln(pos[:, 0], pos[:, 1], y_sorted, xf, route, ln[2], ln[3], layer, alpha, tg)
    return xf.reshape(batch, s, d)
```

```python
import functools
import math

import jax
import jax.numpy as jnp
from jax import lax
from jax.experimental import pallas as pl
from jax.experimental.pallas import tpu as pltpu

BF16 = jnp.bfloat16
F32 = jnp.float32

D_MODEL = 2048
N_HEADS = 8
HEAD_DIM = 128
QK_DIM = 64
SEG = 1024
N_MAIN = 13 * SEG
POOL_WINDOWS = (2, 4, 8, 16)
POOL_GROUP_DIM = 256
POOL_HALO = 16
N_EXPERTS = 8
ROPE_THETA = 10000.0
LN_EPS = 1e-5
LOG2E = math.log2(math.e)
FOX_Q_SCALE = HEAD_DIM ** -0.5 * LOG2E
DIFF_Q_SCALE = QK_DIM ** -0.5 * LOG2E
LANES = 128

MIB = 2 ** 20


def _cparams(semantics, vmem_mib):
    return pltpu.CompilerParams(dimension_semantics=semantics, vmem_limit_bytes=vmem_mib * MIB)


def _rope_store(acc, cos, sin, scale, o_ref):
    lane = lax.broadcasted_iota(jnp.int32, (1, LANES), 1)
    first_half = (lane % QK_DIM) < (QK_DIM // 2)
    for c in range(SEG // LANES):
        a = acc[:, c * LANES:(c + 1) * LANES]
        partner = jnp.where(first_half, pltpu.roll(a, LANES - QK_DIM // 2, 1), pltpu.roll(a, QK_DIM // 2, 1))
        o_ref[:, c * LANES:(c + 1) * LANES] = ((a * cos + partner * sin) * scale).astype(BF16)


def _inproj_kernel(x_ref, w_ref, *rest, mode, first_scale):
    o_ref, wbf_ref = rest[-2:]
    j = pl.program_id(0)
    i = pl.program_id(1)

    @pl.when(i == 0)
    def _cast_weights():
        wbf_ref[...] = w_ref[...].T.astype(BF16)

    acc = jnp.dot(x_ref[...], wbf_ref[...], preferred_element_type=F32)
    scale = jnp.where(j == 0, first_scale, 1.0).astype(F32)
    if mode == "plain":
        o_ref[...] = (acc * scale).astype(BF16)
    elif mode == "rope":
        _rope_store(acc, rest[0][...], rest[1][...], scale, o_ref)
    else:
        o_ref[...] = jax.nn.sigmoid(acc + rest[0][...]).astype(BF16)


PLAIN_SEGS = (0, 1, 2, 5, 6)
ROPE_SEGS = (3, 4)
GATE_SEGS = (7, 8, 9, 10, 11, 12)


def _inproj(x_bf, w_in_t, extras, extra_specs, layer, tm, mode, segs, first_scale):
    s, d = x_bf.shape
    jump = next((k for k in range(1, len(segs)) if segs[k] != segs[k - 1] + 1), len(segs))
    gap = segs[jump] - segs[jump - 1] - 1 if jump < len(segs) else 0
    seg_of = lambda j: segs[0] + j + jnp.where(j >= jump, gap, 0)
    return pl.pallas_call(
        functools.partial(_inproj_kernel, mode=mode, first_scale=first_scale),
        grid=(len(segs), s // tm),
        in_specs=[pl.BlockSpec((tm, d), lambda j, i: (i, 0)),
                  pl.BlockSpec((None, SEG, d), lambda j, i: (layer, seg_of(j), 0))] + extra_specs,
        out_specs=pl.BlockSpec((tm, SEG), lambda j, i: (i, j)),
        out_shape=jax.ShapeDtypeStruct((s, len(segs) * SEG), BF16),
        scratch_shapes=[pltpu.VMEM((d, SEG), BF16)],
        compiler_params=_cparams(("arbitrary", "arbitrary"), 48),
        name="inproj_" + mode,
    )(x_bf, w_in_t, *extras)


def _inproj_all(x_bf, w_in_t, b_gate_flat, cos_t, sin_t, layer, tm):
    table_spec = pl.BlockSpec((tm, LANES), lambda j, i: (i, 0))
    plain = _inproj(x_bf, w_in_t, [], [], layer, tm, "plain", PLAIN_SEGS, FOX_Q_SCALE)
    rope = _inproj(x_bf, w_in_t, [cos_t, sin_t], [table_spec, table_spec], layer, tm, "rope", ROPE_SEGS,
                   DIFF_Q_SCALE)
    gates = _inproj(x_bf, w_in_t, [b_gate_flat], [pl.BlockSpec((None, 1, SEG), lambda j, i: (layer, 0, j))],
                    layer, tm, "gate", GATE_SEGS, 1.0)
    return plain, rope, gates


def _split3(v):
    hi = v.astype(BF16)
    r = v - hi.astype(F32)
    mid = r.astype(BF16)
    lo = (r - mid.astype(F32)).astype(BF16)
    return hi, mid, lo


def _forget_prep_kernel(x_ref, wfg_ref, bf_ref, fq_ref, fk_ref, qa_ref, ka_ref, carry_ref, *, tm):
    i = pl.program_id(0)

    @pl.when(i == 0)
    def _init():
        carry_ref[...] = jnp.zeros_like(carry_ref)

    w_fg = jnp.concatenate([wfg_ref[...], jnp.zeros((LANES - N_HEADS, wfg_ref.shape[1]), F32)], axis=0)
    z = lax.dot_general(x_ref[...], w_fg.astype(BF16), (((1,), (1,)), ((), ())),
                        preferred_element_type=F32) + bf_ref[...]
    log_f = -(jnp.maximum(-z, 0.0) + jnp.log1p(jnp.exp(-jnp.abs(z)))) * LOG2E
    row = lax.broadcasted_iota(jnp.int32, (tm, tm), 0)
    col = lax.broadcasted_iota(jnp.int32, (tm, tm), 1)
    tri = (col <= row).astype(BF16)
    hi, mid, lo = _split3(log_f)
    cum = (jnp.dot(tri, hi, preferred_element_type=F32) + jnp.dot(tri, mid, preferred_element_type=F32)
           + jnp.dot(tri, lo, preferred_element_type=F32)) + carry_ref[...]
    carry_ref[...] = cum[tm - 1:tm, :]
    c_hi, c_mid, c_lo = (p.astype(F32) for p in _split3(cum))
    lane = lax.broadcasted_iota(jnp.int32, (1, LANES), 1)
    for h in range(N_HEADS):
        ch, cm, cl = c_hi[:, h:h + 1], c_mid[:, h:h + 1], c_lo[:, h:h + 1]
        q_extra = jnp.where(lane == 0, ch, jnp.where(lane == 1, cm, jnp.where(lane == 2, cl,
                  jnp.where(lane < 6, 1.0, 0.0))))
        k_extra = jnp.where(lane < 3, 1.0, jnp.where(lane == 3, -ch, jnp.where(lane == 4, -cm,
                  jnp.where(lane == 5, -cl, 0.0))))
        q_aug = jnp.concatenate([fq_ref[:, h * HEAD_DIM:(h + 1) * HEAD_DIM].astype(F32), q_extra], axis=1)
        qa_ref[h] = q_aug.T.astype(BF16)
        ka_ref[h, :, 0:HEAD_DIM] = fk_ref[:, h * HEAD_DIM:(h + 1) * HEAD_DIM]
        ka_ref[h, :, HEAD_DIM:2 * HEAD_DIM] = k_extra.astype(BF16)


def _forget_prep(x_bf, w_in_t, b_fg, proj, layer, tm, tq):
    s, d = x_bf.shape
    per_tile = tq // tm
    return pl.pallas_call(
        functools.partial(_forget_prep_kernel, tm=tm),
        grid=(s // tm,),
        in_specs=[
            pl.BlockSpec((tm, d), lambda i: (i, 0)),
            pl.BlockSpec((None, N_HEADS, d), lambda i: (layer, N_MAIN // N_HEADS, 0)),
            pl.BlockSpec((1, LANES), lambda i: (0, 0)),
            pl.BlockSpec((tm, SEG), lambda i: (i, 0)),
            pl.BlockSpec((tm, SEG), lambda i: (i, 1)),
        ],
        out_specs=[pl.BlockSpec((N_HEADS, None, 2 * HEAD_DIM, tm),
                                lambda i: (0, i // per_tile, 0, i % per_tile)),
                   pl.BlockSpec((N_HEADS, tm, 2 * HEAD_DIM), lambda i: (0, i, 0))],
        out_shape=[jax.ShapeDtypeStruct((N_HEADS, s // tq, 2 * HEAD_DIM, tq), BF16),
                   jax.ShapeDtypeStruct((N_HEADS, s, 2 * HEAD_DIM), BF16)],
        scratch_shapes=[pltpu.VMEM((1, LANES), F32)],
        compiler_params=_cparams(("arbitrary",), 32),
        name="forget_prep",
    )(x_bf, w_in_t, b_fg, proj, proj)


def _transpose_kernel(fox_v_ref, diff_q_ref, diff_v_ref, o_ref, *, tk):
    for g, ref in enumerate((fox_v_ref, diff_q_ref, diff_v_ref)):
        xt = ref[...].astype(F32).T
        for c in range(o_ref.shape[1]):
            o_ref[g, c] = xt[:, c * tk:(c + 1) * tk].astype(BF16)


def _transpose_segs(plain, rope, tm, tk):
    s = plain.shape[0]
    seg_spec = lambda blk: pl.BlockSpec((tm, SEG), lambda i: (i, blk))
    return pl.pallas_call(
        functools.partial(_transpose_kernel, tk=tk),
        grid=(s // tm,),
        in_specs=[seg_spec(PLAIN_SEGS.index(2)), seg_spec(ROPE_SEGS.index(3)), seg_spec(PLAIN_SEGS.index(5))],
        out_specs=pl.BlockSpec((3, tm // tk, SEG, tk), lambda i: (0, i, 0, 0)),
        out_shape=jax.ShapeDtypeStruct((3, s // tk, SEG, tk), BF16),
        compiler_params=_cparams(("arbitrary",), 40),
        name="transpose_segs",
    )(plain, rope, plain)


ONES_ROWS = 16


class _AttnState:
    S_SLOTS = 4
    P_SLOTS = 2
    N_REFS = S_SLOTS + 2 * P_SLOTS + 2

    def __init__(self, refs):
        a, b = self.S_SLOTS, self.S_SLOTS + self.P_SLOTS
        self.s = refs[0:a]
        self.p = refs[a:b]
        self.alpha = refs[b:b + self.P_SLOTS]
        self.m, self.acc = refs[b + self.P_SLOTS:]

    @classmethod
    def scratch(cls, tq, tk):
        row = pltpu.VMEM((1, tq), F32)
        return ([pltpu.VMEM((tk, tq), F32)] * cls.S_SLOTS + [pltpu.VMEM((tk, tq), BF16)] * cls.P_SLOTS
                + [row] * cls.P_SLOTS + [row, pltpu.VMEM((HEAD_DIM + ONES_ROWS, tq), F32)])

    def init(self):
        self.m[...] = jnp.full_like(self.m, -jnp.inf)
        self.acc[...] = jnp.zeros_like(self.acc)
        self.p[1][...] = jnp.zeros_like(self.p[1])
        self.alpha[1][...] = jnp.ones_like(self.alpha[1])

    def normalized(self):
        return self.acc[0:HEAD_DIM, :] / self.acc[HEAD_DIM:HEAD_DIM + 1, :]


def _attn_pipeline(i, q_t, k_ref, v_t_ref, states, tq, tk):
    n_sub = tq // tk
    assert tq == n_sub * tk and n_sub == _AttnState.S_SLOTS

    def scores(c, slot, q0=0):
        k = k_ref[pl.ds(pl.multiple_of(c * tk, tk), tk), :]
        for st, q in zip(states, q_t):
            st.s[slot][:, q0:] = jnp.dot(k, q[:, q0:], preferred_element_type=F32)

    def softmax(s_slot, slot, k0=None, q0=0):
        for st in states:
            s = st.s[s_slot][:, q0:]
            if k0 is not None:
                key = lax.broadcasted_iota(jnp.int32, s.shape, 0) + k0
                query = lax.broadcasted_iota(jnp.int32, s.shape, 1) + (i * tq + q0)
                s = jnp.where(key <= query, s, -jnp.inf)
            m_prev = st.m[:, q0:]
            m_new = jnp.maximum(m_prev, jnp.max(s, axis=0, keepdims=True))
            st.m[:, q0:] = m_new
            st.alpha[slot][:, q0:] = jnp.exp2(m_prev - m_new)
            st.p[slot][:, q0:] = jnp.exp2(s - m_new).astype(BF16)

    def values(c, slot, q0=0):
        v_t = jnp.concatenate([v_t_ref[c], jnp.ones((ONES_ROWS, tk), BF16)], axis=0)
        for st in states:
            st.acc[:, q0:] = st.alpha[slot][:, q0:] * st.acc[:, q0:] + jnp.dot(
                v_t, st.p[slot][:, q0:], preferred_element_type=F32)

    for st in states:
        st.init()
    scores(0, 0)
    scores(1, 1)

    def body(block, carry):
        c = n_sub * block
        for u in range(n_sub):
            scores(c + u + 2, (u + 2) % n_sub)
            softmax(u, u % 2)
            values(jnp.maximum(c + u - 1, 0), (u + 1) % 2)
        return carry

    lax.fori_loop(0, i, body, 0)
    c = n_sub * i
    for u in range(n_sub):
        if u + 2 < n_sub:
            scores(c + u + 2, u + 2, q0=(u + 2) * tk)
        softmax(u, u % 2, k0=(c + u) * tk, q0=u * tk)
        values(jnp.maximum(c + u - 1, 0), (u + 1) % 2, q0=max(u - 1, 0) * tk)
    values(c + n_sub - 1, (n_sub - 1) % 2, q0=(n_sub - 1) * tk)


def _fox_attn_kernel(q_ref, k_ref, v_ref, o_ref, *scratch, tq, tk):
    st = _AttnState(scratch)
    _attn_pipeline(pl.program_id(1), [q_ref[...]], k_ref, v_ref, [st], tq, tk)
    o_ref[...] = st.normalized().T.astype(BF16)


def _fox_attn(q_aug_t, k_aug, segs_t, tq, tk):
    _, s, _ = k_aug.shape
    return pl.pallas_call(
        functools.partial(_fox_attn_kernel, tq=tq, tk=tk),
        grid=(N_HEADS, s // tq),
        in_specs=[
            pl.BlockSpec((None, None, 2 * HEAD_DIM, tq), lambda h, i: (h, i, 0, 0)),
            pl.BlockSpec((None, s, 2 * HEAD_DIM), lambda h, i: (h, 0, 0)),
            pl.BlockSpec((None, s // tk, HEAD_DIM, tk), lambda h, i: (0, 0, h, 0)),
        ],
        out_specs=pl.BlockSpec((tq, HEAD_DIM), lambda h, i: (i, h)),
        out_shape=jax.ShapeDtypeStruct((s, SEG), BF16),
        scratch_shapes=_AttnState.scratch(tq, tk),
        compiler_params=_cparams(("arbitrary", "arbitrary"), 40),
        name="fox_attn",
    )(q_aug_t, k_aug, segs_t)


def _diff_attn_kernel(q_ref, k_ref, v_ref, dl_ref, gain_ref, o_ref, *scratch, tq, tk, lambda_init):
    st1 = _AttnState(scratch[:_AttnState.N_REFS])
    st2 = _AttnState(scratch[_AttnState.N_REFS:])
    q_t = jnp.concatenate([q_ref[c] for c in range(tq // tk)], axis=1)
    channel = lax.broadcasted_iota(jnp.int32, (HEAD_DIM, 1), 0)
    q1 = jnp.where(channel < QK_DIM, q_t, jnp.zeros_like(q_t))
    q2 = jnp.where(channel >= QK_DIM, q_t, jnp.zeros_like(q_t))
    _attn_pipeline(pl.program_id(1), [q1, q2], k_ref, v_ref, [st1, st2], tq, tk)

    dl = dl_ref[...]
    lam = (jnp.exp(jnp.sum(dl[0:1, :] * dl[1:2, :], axis=1, keepdims=True))
           - jnp.exp(jnp.sum(dl[2:3, :] * dl[3:4, :], axis=1, keepdims=True)) + lambda_init)
    o = st1.normalized() - lam * st2.normalized()
    o = o * lax.rsqrt(jnp.mean(o * o, axis=0, keepdims=True) + LN_EPS)
    o_ref[...] = (o.T * gain_ref[...] * (1.0 - lambda_init)).astype(BF16)


def _diff_attn(rope, segs_t, diff_lambda, diff_gain, layer, lambda_init, tq, tk):
    s = rope.shape[0]
    k_block0 = ROPE_SEGS.index(4) * SEG // HEAD_DIM
    return pl.pallas_call(
        functools.partial(_diff_attn_kernel, tq=tq, tk=tk, lambda_init=lambda_init),
        grid=(N_HEADS, s // tq),
        in_specs=[
            pl.BlockSpec((None, tq // tk, HEAD_DIM, tk), lambda h, i: (1, i, h, 0)),
            pl.BlockSpec((s, HEAD_DIM), lambda h, i: (0, k_block0 + h)),
            pl.BlockSpec((None, s // tk, HEAD_DIM, tk), lambda h, i: (2, 0, h, 0)),
            pl.BlockSpec((None, 4, QK_DIM), lambda h, i: (layer, 0, 0)),
            pl.BlockSpec((None, 1, HEAD_DIM), lambda h, i: (layer, 0, 0)),
        ],
        out_specs=pl.BlockSpec((tq, HEAD_DIM), lambda h, i: (i, h)),
        out_shape=jax.ShapeDtypeStruct((s, SEG), BF16),
        scratch_shapes=_AttnState.scratch(tq, tk) * 2,
        compiler_params=_cparams(("arbitrary", "arbitrary"), 40),
        name="diff_attn",
    )(segs_t, rope, segs_t, diff_lambda, diff_gain)


def _pool_kernel(u_ref, halo_ref, wg_ref, sc_ref, o_ref, *, tm):
    i = pl.program_id(0)
    tokens_seen = (lax.broadcasted_iota(jnp.int32, (tm, 1), 0) + i * tm + 1).astype(F32)
    for g, window in enumerate(POOL_WINDOWS):
        cols = slice(g * POOL_GROUP_DIM, (g + 1) * POOL_GROUP_DIM)
        u = u_ref[:, cols].astype(F32)
        halo = halo_ref[:, cols].astype(F32)
        halo = jnp.where(i > 0, halo, jnp.zeros_like(halo))
        ext = jnp.concatenate([halo, u], axis=0)
        shift = 1
        while shift < window:
            ext = ext + pltpu.roll(ext, shift, 0)
            shift *= 2
        pooled = ext[POOL_HALO:, :] / jnp.minimum(tokens_seen, float(window))
        delta = (pooled - u).astype(BF16)
        y = jnp.dot(delta, wg_ref[g].astype(BF16), preferred_element_type=F32) * sc_ref[:, cols]
        o_ref[:, cols] = y.astype(BF16)


def _pool(plain, w_pool_group, pool_scale, layer, tm):
    s = plain.shape[0]
    n_groups = len(POOL_WINDOWS)
    halo_blocks_per_tile = tm // POOL_HALO
    blk = PLAIN_SEGS.index(6)
    return pl.pallas_call(
        functools.partial(_pool_kernel, tm=tm),
        grid=(s // tm,),
        in_specs=[
            pl.BlockSpec((tm, SEG), lambda i: (i, blk)),
            pl.BlockSpec((POOL_HALO, SEG), lambda i: (jnp.maximum(i * halo_blocks_per_tile - 1, 0), blk)),
            pl.BlockSpec((None, n_groups, POOL_GROUP_DIM, POOL_GROUP_DIM), lambda i: (layer, 0, 0, 0)),
            pl.BlockSpec((None, 1, SEG), lambda i: (layer, 0, 0)),
        ],
        out_specs=pl.BlockSpec((tm, SEG), lambda i: (i, 0)),
        out_shape=jax.ShapeDtypeStruct((s, SEG), BF16),
        compiler_params=_cparams(("arbitrary",), 32),
        name="pool",
    )(plain, plain, w_pool_group, pool_scale)


def _merge_kernel(ya_ref, yb_ref, yc_ref, wa_ref, wb_ref, wc_ref, ga_ref, gb_ref, gc_ref, o_ref,
                  wab_ref, wbb_ref, wcb_ref):
    i = pl.program_id(1)

    @pl.when(i == 0)
    def _cast_weights():
        wab_ref[...] = wa_ref[...].astype(BF16)
        wbb_ref[...] = wb_ref[...].astype(BF16)
        wcb_ref[...] = wc_ref[...].astype(BF16)

    h = ga_ref[...].astype(F32) * jnp.dot(ya_ref[...], wab_ref[...], preferred_element_type=F32)
    h = h + gb_ref[...].astype(F32) * jnp.dot(yb_ref[...], wbb_ref[...], preferred_element_type=F32)
    h = h + gc_ref[...].astype(F32) * jnp.dot(yc_ref[...], wcb_ref[...], preferred_element_type=F32)
    o_ref[...] = h.astype(BF16)


def _merge(y_a, y_b, y_c, w_a, w_b, w_c, gates, layer, tm, tn):
    s = y_a.shape[0]
    gate_blocks = D_MODEL // tn
    y_spec = pl.BlockSpec((tm, SEG), lambda j, i: (i, 0))
    w_spec = pl.BlockSpec((None, SEG, tn), lambda j, i: (layer, 0, j))
    gate_spec = lambda br: pl.BlockSpec((tm, tn), lambda j, i: (i, br * gate_blocks + j))
    return pl.pallas_call(
        _merge_kernel,
        grid=(D_MODEL // tn, s // tm),
        in_specs=[y_spec, y_spec, y_spec, w_spec, w_spec, w_spec, gate_spec(0), gate_spec(1), gate_spec(2)],
        out_specs=pl.BlockSpec((tm, tn), lambda j, i: (i, j)),
        out_shape=jax.ShapeDtypeStruct((s, D_MODEL), BF16),
        scratch_shapes=[pltpu.VMEM((SEG, tn), BF16)] * 3,
        compiler_params=_cparams(("arbitrary", "arbitrary"), 48),
        name="merge",
    )(y_a, y_b, y_c, w_a, w_b, w_c, gates, gates, gates)


def _matmul_kernel(x_ref, w_ref, o_ref, wbf_ref):
    @pl.when(pl.program_id(1) == 0)
    def _cast_weights():
        wbf_ref[...] = w_ref[...].astype(BF16)

    o_ref[...] = jnp.dot(x_ref[...], wbf_ref[...], preferred_element_type=F32).astype(o_ref.dtype)


def _matmul(x_bf, w, layer, tm, tn, out_dtype):
    s, k = x_bf.shape
    n = w.shape[-1]
    return pl.pallas_call(
        _matmul_kernel,
        grid=(n // tn, s // tm),
        in_specs=[pl.BlockSpec((tm, k), lambda j, i: (i, 0)),
                  pl.BlockSpec((None, k, tn), lambda j, i: (layer, 0, j))],
        out_specs=pl.BlockSpec((tm, tn), lambda j, i: (i, j)),
        out_shape=jax.ShapeDtypeStruct((s, n), out_dtype),
        scratch_shapes=[pltpu.VMEM((k, tn), BF16)],
        compiler_params=_cparams(("arbitrary", "arbitrary"), 48),
        name="matmul",
    )(x_bf, w)


def _layer_norm_rows(z, g, b):
    mu = jnp.mean(z, axis=1, keepdims=True)
    zc = z - mu
    var = jnp.mean(zc * zc, axis=1, keepdims=True)
    return zc * lax.rsqrt(var + LN_EPS) * g + b


def _add_ln_kernel(x_ref, f_ref, g_ref, b_ref, xo_ref, xb_ref, *, alpha):
    y = _layer_norm_rows(alpha * x_ref[...] + f_ref[...], g_ref[...], b_ref[...])
    xo_ref[...] = y
    xb_ref[...] = y.astype(BF16)


def _add_ln(x, f, g, b, layer, alpha, tm):
    s, d = x.shape
    row_spec = pl.BlockSpec((tm, d), lambda i: (i, 0))
    par_spec = pl.BlockSpec((None, 1, d), lambda i: (layer, 0, 0))
    return pl.pallas_call(
        functools.partial(_add_ln_kernel, alpha=alpha),
        grid=(s // tm,),
        in_specs=[row_spec, row_spec, par_spec, par_spec],
        out_specs=[row_spec, row_spec],
        out_shape=[jax.ShapeDtypeStruct((s, d), F32), jax.ShapeDtypeStruct((s, d), BF16)],
        compiler_params=_cparams(("arbitrary",), 32),
        name="add_ln",
    )(x, f, g, b)


def _gate_up_kernel(te_ref, tf_ref, nu_ref, x_ref, wg_ref, wu_ref, o_ref, wgb_ref, wub_ref):
    t = pl.program_id(1)

    @pl.when(tf_ref[t] == 1)
    def _cast_weights():
        wgb_ref[...] = wg_ref[...].astype(BF16)
        wub_ref[...] = wu_ref[...].astype(BF16)

    @pl.when(t < nu_ref[0])
    def _compute():
        x = x_ref[...]
        g = jnp.dot(x, wgb_ref[...], preferred_element_type=F32)
        u = jnp.dot(x, wub_ref[...], preferred_element_type=F32)
        o_ref[...] = (g * jax.nn.sigmoid(g) * u).astype(BF16)

    @pl.when(t >= nu_ref[0])
    def _unused_tile():
        o_ref[...] = jnp.zeros_like(o_ref)


def _gate_up(groups, x_sorted, w_gate, w_up, layer, tm, tn):
    tile_expert, tile_first, n_used = groups
    n_slots, k = x_sorted.shape
    n = w_gate.shape[-1]
    row = lambda j, t, te, tf, nu: (jnp.minimum(t, nu[0] - 1), 0)
    w_spec = pl.BlockSpec((None, None, k, tn), lambda j, t, te, tf, nu: (layer, te[t], 0, j))
    return pl.pallas_call(
        _gate_up_kernel,
        grid_spec=pltpu.PrefetchScalarGridSpec(
            num_scalar_prefetch=3,
            grid=(n // tn, n_slots // tm),
            in_specs=[pl.BlockSpec((tm, k), row), w_spec, w_spec],
            out_specs=pl.BlockSpec((tm, tn), lambda j, t, te, tf, nu: (t, j)),
            scratch_shapes=[pltpu.VMEM((k, tn), BF16)] * 2,
        ),
        out_shape=jax.ShapeDtypeStruct((n_slots, n), BF16),
        compiler_params=_cparams(("arbitrary", "arbitrary"), 48),
        name="ffn_gate_up",
    )(tile_expert, tile_first, n_used, x_sorted, w_gate, w_up)


def _down_kernel(te_ref, tf_ref, nu_ref, h_ref, w_ref, o_ref, wbf_ref):
    t = pl.program_id(1)

    @pl.when(tf_ref[t] == 1)
    def _cast_weights():
        wbf_ref[...] = w_ref[...].astype(BF16)

    @pl.when(t < nu_ref[0])
    def _compute():
        o_ref[...] = jnp.dot(h_ref[...], wbf_ref[...], preferred_element_type=F32)

    @pl.when(t >= nu_ref[0])
    def _unused_tile():
        o_ref[...] = jnp.zeros_like(o_ref)


def _down(groups, h_sorted, w_down, layer, tm, tn):
    tile_expert, tile_first, n_used = groups
    n_slots, k = h_sorted.shape
    n = w_down.shape[-1]
    return pl.pallas_call(
        _down_kernel,
        grid_spec=pltpu.PrefetchScalarGridSpec(
            num_scalar_prefetch=3,
            grid=(n // tn, n_slots // tm),
            in_specs=[pl.BlockSpec((tm, k), lambda j, t, te, tf, nu: (jnp.minimum(t, nu[0] - 1), 0)),
                      pl.BlockSpec((None, None, k, tn), lambda j, t, te, tf, nu: (layer, te[t], 0, j))],
            out_specs=pl.BlockSpec((tm, tn), lambda j, t, te, tf, nu: (t, j)),
            scratch_shapes=[pltpu.VMEM((k, tn), BF16)],
        ),
        out_shape=jax.ShapeDtypeStruct((n_slots, n), F32),
        compiler_params=_cparams(("arbitrary", "arbitrary"), 52),
        name="ffn_down",
    )(tile_expert, tile_first, n_used, h_sorted, w_down)


def _single_group(n_tiles):
    return (jnp.zeros((n_tiles,), jnp.int32),
            jnp.zeros((n_tiles,), jnp.int32).at[0].set(1),
            jnp.full((1,), n_tiles, jnp.int32))


def _router_kernel(x_ref, rw_ref, rb_ref, o_ref, cnt_ref, carry_ref, *, tm):
    i = pl.program_id(0)

    @pl.when(i == 0)
    def _init():
        carry_ref[...] = jnp.zeros_like(carry_ref)

    logits = jnp.dot(x_ref[...], rw_ref[...], preferred_element_type=F32,
                     precision=lax.Precision.HIGHEST) + rb_ref[...]
    lane = lax.broadcasted_iota(jnp.int32, (tm, LANES), 1).astype(F32)
    lg = jnp.where(lane < N_EXPERTS, logits, -jnp.inf)
    v1 = jnp.max(lg, axis=1, keepdims=True)
    e1 = jnp.min(jnp.where(lg == v1, lane, float(LANES)), axis=1, keepdims=True)
    lg2 = jnp.where(lane == e1, -jnp.inf, lg)
    v2 = jnp.max(lg2, axis=1, keepdims=True)
    e2 = jnp.min(jnp.where(lg2 == v2, lane, float(LANES)), axis=1, keepdims=True)
    t2 = jnp.exp(v2 - v1)
    w1 = 1.0 / (1.0 + t2)
    w2 = t2 / (1.0 + t2)
    pick1 = lane == e1
    pick2 = lane == e2
    onehot = jnp.where(pick1 | pick2, 1.0, 0.0)
    row = lax.broadcasted_iota(jnp.int32, (tm, tm), 0)
    col = lax.broadcasted_iota(jnp.int32, (tm, tm), 1)
    before = (col < row).astype(BF16)
    seen = jnp.dot(before, onehot.astype(BF16), preferred_element_type=F32) + carry_ref[...]
    r1 = jnp.sum(jnp.where(pick1, seen, 0.0), axis=1, keepdims=True)
    r2 = jnp.sum(jnp.where(pick2, seen, 0.0), axis=1, keepdims=True)
    carry_ref[...] = carry_ref[...] + jnp.sum(onehot, axis=0, keepdims=True)
    o_ref[...] = jnp.where(lane == 0, e1, jnp.where(lane == 1, e2, jnp.where(lane == 2, w1,
                 jnp.where(lane == 3, w2, jnp.where(lane == 4, r1, jnp.where(lane == 5, r2, 0.0))))))
    cnt_ref[...] = carry_ref[...]


def _router(x, rw_pad, rb_pad, tm):
    s, d = x.shape
    return pl.pallas_call(
        functools.partial(_router_kernel, tm=tm),
        grid=(s // tm,),
        in_specs=[pl.BlockSpec((tm, d), lambda i: (i, 0)),
                  pl.BlockSpec((d, LANES), lambda i: (0, 0)),
                  pl.BlockSpec((1, LANES), lambda i: (0, 0))],
        out_specs=[pl.BlockSpec((tm, LANES), lambda i: (i, 0)), pl.BlockSpec((1, LANES), lambda i: (0, 0))],
        out_shape=[jax.ShapeDtypeStruct((s, LANES), F32), jax.ShapeDtypeStruct((1, LANES), F32)],
        scratch_shapes=[pltpu.VMEM((1, LANES), F32)],
        compiler_params=_cparams(("arbitrary",), 32),
        name="router",
    )(x, rw_pad, rb_pad)


def _scatter_copies(x_hbm, o_hbm, p1_ref, p2_ref, t, sem):
    return (pltpu.make_async_copy(x_hbm.at[t], o_hbm.at[p1_ref[t]], sem),
            pltpu.make_async_copy(x_hbm.at[t], o_hbm.at[p2_ref[t]], sem))


def _dispatch_kernel(p1_ref, p2_ref, x_hbm, init_hbm, o_hbm, sem, *, tt):
    del init_hbm
    base = pl.program_id(0) * tt

    def issue(r, carry):
        for cp in _scatter_copies(x_hbm, o_hbm, p1_ref, p2_ref, base + r, sem):
            cp.start()
        return carry

    lax.fori_loop(0, tt, issue, 0, unroll=8)

    def drain(r, carry):
        for cp in _scatter_copies(x_hbm, o_hbm, p1_ref, p2_ref, base + r, sem):
            cp.wait()
        return carry

    lax.fori_loop(0, tt, drain, 0, unroll=8)


def _dispatch(pos1, pos2, x_bf, n_slots, tt):
    s, d = x_bf.shape
    x_slabs = x_bf.reshape(s, d // LANES, LANES)
    out = pl.pallas_call(
        functools.partial(_dispatch_kernel, tt=tt),
        grid_spec=pltpu.PrefetchScalarGridSpec(
            num_scalar_prefetch=2,
            grid=(s // tt,),
            in_specs=[pl.BlockSpec(memory_space=pl.ANY), pl.BlockSpec(memory_space=pl.ANY)],
            out_specs=pl.BlockSpec(memory_space=pl.ANY),
            scratch_shapes=[pltpu.SemaphoreType.DMA],
        ),
        out_shape=jax.ShapeDtypeStruct((n_slots, d // LANES, LANES), BF16),
        input_output_aliases={3: 0},
        compiler_params=_cparams(("arbitrary",), 32),
        name="moe_dispatch",
    )(pos1, pos2, x_slabs, jnp.zeros((n_slots, d // LANES, LANES), BF16))
    return out.reshape(n_slots, d)


def _row_copy(src_hbm, src_row, buf, dst_row, sem):
    return pltpu.make_async_copy(src_hbm.at[pl.ds(src_row, 1), :], buf.at[pl.ds(dst_row, 1), :], sem)


def _combine_ln_kernel(p1_ref, p2_ref, y_hbm, x_ref, route_ref, g_ref, b_ref, xo_ref, xb_ref,
                       buf, sem, *, tc, alpha):
    i = pl.program_id(0)

    def copies(step, slot, r):
        t = step * tc + r
        return (_row_copy(y_hbm, p1_ref[t], buf.at[slot, 0], r, sem.at[slot]),
                _row_copy(y_hbm, p2_ref[t], buf.at[slot, 1], r, sem.at[slot]))

    def issue(step, slot):
        def body(r, carry):
            for cp in copies(step, slot, r):
                cp.start()
            return carry

        lax.fori_loop(0, tc, body, 0, unroll=8)

    @pl.when(i == 0)
    def _first():
        issue(0, 0)

    @pl.when(i + 1 < pl.num_programs(0))
    def _prefetch():
        issue(i + 1, (i + 1) % 2)

    slot = i % 2

    def drain(r, carry):
        for cp in copies(i, slot, r):
            cp.wait()
        return carry

    lax.fori_loop(0, tc, drain, 0, unroll=8)
    route = route_ref[...]
    f = route[:, 2:3] * buf[slot, 0] + route[:, 3:4] * buf[slot, 1]
    y = _layer_norm_rows(alpha * x_ref[...] + f, g_ref[...], b_ref[...])
    xo_ref[...] = y
    xb_ref[...] = y.astype(BF16)


def _combine_ln(pos1, pos2, y_sorted, x, route, g, b, layer, alpha, tc):
    s, d = x.shape
    row_spec = pl.BlockSpec((tc, d), lambda i, p1, p2: (i, 0))
    par_spec = pl.BlockSpec((None, 1, d), lambda i, p1, p2: (layer, 0, 0))
    return pl.pallas_call(
        functools.partial(_combine_ln_kernel, tc=tc, alpha=alpha),
        grid_spec=pltpu.PrefetchScalarGridSpec(
            num_scalar_prefetch=2,
            grid=(s // tc,),
            in_specs=[pl.BlockSpec(memory_space=pl.ANY), row_spec,
                      pl.BlockSpec((tc, LANES), lambda i, p1, p2: (i, 0)), par_spec, par_spec],
            out_specs=[row_spec, row_spec],
            scratch_shapes=[pltpu.VMEM((2, 2, tc, d), F32), pltpu.SemaphoreType.DMA((2,))],
        ),
        out_shape=[jax.ShapeDtypeStruct((s, d), F32), jax.ShapeDtypeStruct((s, d), BF16)],
        compiler_params=_cparams(("arbitrary",), 32),
        name="moe_combine_ln",
    )(pos1, pos2, y_sorted, x, route, g, b)


def _moe_plan(route, counts, tm, n_tiles):
    expert = route[:, 0:2].astype(jnp.int32)
    rank = route[:, 4:6].astype(jnp.int32)
    count = counts[0, :N_EXPERTS].astype(jnp.int32)
    padded = ((count + tm - 1) // tm) * tm
    ends = jnp.cumsum(padded)
    starts = ends - padded
    pos = starts[expert] + rank
    n_used = ends[-1] // tm
    tile = jnp.arange(n_tiles, dtype=jnp.int32)
    last = jnp.minimum(tile, n_used - 1)
    tile_expert = jnp.sum(last[:, None] * tm >= ends[None, :], axis=1).astype(jnp.int32)
    prev = jnp.concatenate([jnp.full((1,), -1, jnp.int32), tile_expert[:-1]])
    tile_first = (tile_expert != prev).astype(jnp.int32)
    return pos, (tile_expert, tile_first, n_used.reshape(1).astype(jnp.int32))


def _rope_tables(seq):
    pos = jnp.arange(seq, dtype=F32)
    inv = ROPE_THETA ** (-jnp.arange(0, QK_DIM, 2, dtype=F32) / QK_DIM)
    ang = pos[:, None] * inv[None, :]
    cos, sin = jnp.cos(ang), jnp.sin(ang)
    return jnp.tile(cos, (1, 4)), jnp.tile(jnp.concatenate([-sin, sin], axis=1), (1, 2))


def _pad_lanes(a):
    return jnp.pad(a, ((0, 0), (0, LANES - a.shape[1])))


def kernel(x, w_in, b_forget, w_pool_group, pool_scale, diff_lambda, diff_norm_gain, w_branch_a, w_branch_b, w_branch_c, b_gate, w_out, ln1_g, ln1_b, ln2_g, ln2_b, ffn_w_gate, ffn_w_up, ffn_w_down, router_w, router_b, expert_w_gate, expert_w_up, expert_w_down):
    batch, s, d = x.shape
    assert batch == 1 and d == D_MODEL and w_in.shape[-1] == N_MAIN + N_HEADS
    depth = w_in.shape[0]
    alpha = (2 * depth) ** 0.25

    tm = min(1024, s)
    tq = min(1024, s)
    tk = tq // 4
    te = min(512, s)
    tg = min(256, s)
    n_ff = 512
    n_tiles = 2 * s // te + N_EXPERTS
    n_slots = n_tiles * te

    cos_t, sin_t = _rope_tables(s)
    w_in_t = jnp.swapaxes(w_in, 1, 2)
    b_gate_flat = b_gate.reshape(depth, 1, 3 * D_MODEL)
    pool_scale3 = pool_scale.reshape(depth, 1, SEG)
    gain3 = diff_norm_gain.reshape(depth, 1, HEAD_DIM)
    ln = [p.reshape(depth, 1, D_MODEL) for p in (ln1_g, ln1_b, ln2_g, ln2_b)]
    dense_w = [w.reshape(w.shape[0], 1, *w.shape[1:]) for w in (ffn_w_gate, ffn_w_up, ffn_w_down)]

    xf = x.reshape(s, d)
    xb = xf.astype(BF16)
    for layer in range(depth):
        lambda_init = 0.8 - 0.6 * math.exp(-0.3 * layer)
        plain, rope, gates = _inproj_all(xb, w_in_t, b_gate_flat, cos_t, sin_t, layer, tm)
        b_fg = _pad_lanes(b_forget[layer].reshape(1, N_HEADS))
        q_aug_t, k_aug = _forget_prep(xb, w_in_t, b_fg, plain, layer, te, tq)
        segs_t = _transpose_segs(plain, rope, te, tk)
        y_a = _fox_attn(q_aug_t, k_aug, segs_t, tq, tk)
        y_b = _pool(plain, w_pool_group, pool_scale3, layer, tm)
        y_c = _diff_attn(rope, segs_t, diff_lambda, gain3, layer, lambda_init, tq, tk)
        h = _merge(y_a, y_b, y_c, w_branch_a, w_branch_b, w_branch_c, gates, layer, tm, 512)
        mix = _matmul(h, w_out, layer, tm, SEG, F32)
        xf, xb = _add_ln(xf, mix, ln[0], ln[1], layer, alpha, tg)

        j = layer // 2
        if layer % 2 == 0:
            groups = _single_group(s // te)
            hidden = _gate_up(groups, xb, dense_w[0], dense_w[1], j, te, n_ff)
            f = _down(groups, hidden, dense_w[2], j, te, 512)
            xf, xb = _add_ln(xf, f, ln[2], ln[3], layer, alpha, tg)
        else:
            route, counts = _router(xf, _pad_lanes(router_w[j]), _pad_lanes(router_b[j].reshape(1, N_EXPERTS)), te)
            pos, groups = _moe_plan(route, counts, te, n_tiles)
            x_sorted = _dispatch(pos[:, 0], pos[:, 1], xb, n_slots, te)
            hidden = _gate_up(groups, x_sorted, expert_w_gate, expert_w_up, j, te, n_ff)
            y_sorted = _down(groups, hidden, expert_w_down, j, te, 512)
            xf, xb = _combine_ln(pos[:, 0], pos[:, 1], y_sorted, xf, route, ln[2], ln[3], layer, alpha, tg)
    return xf.reshape(batch, s, d)
```

```python
import functools
import math

import jax
import jax.numpy as jnp
from jax import lax
from jax.experimental import pallas as pl
from jax.experimental.pallas import tpu as pltpu

BF16 = jnp.bfloat16
F32 = jnp.float32

D_MODEL = 2048
N_HEADS = 8
HEAD_DIM = 128
QK_DIM = 64
SEG = 1024
N_MAIN = 13 * SEG
POOL_WINDOWS = (2, 4, 8, 16)
POOL_GROUP_DIM = 256
POOL_HALO = 16
N_EXPERTS = 8
ROPE_THETA = 10000.0
LN_EPS = 1e-5
LOG2E = math.log2(math.e)
FOX_Q_SCALE = HEAD_DIM ** -0.5 * LOG2E
DIFF_Q_SCALE = QK_DIM ** -0.5 * LOG2E
LANES = 128

MIB = 2 ** 20


def _cparams(semantics, vmem_mib):
    return pltpu.CompilerParams(dimension_semantics=semantics, vmem_limit_bytes=vmem_mib * MIB)


def _with_bf16_weights(fresh, cached, w_refs, wbf_refs, emit, transpose=False):
    @pl.when(fresh)
    def _fresh():
        ws = [(w[...].T if transpose else w[...]).astype(BF16) for w in w_refs]
        for dst, w in zip(wbf_refs, ws):
            dst[...] = w
        emit(ws)

    @pl.when(cached)
    def _cached():
        emit([r[...] for r in wbf_refs])


def _rope_store(acc, cos, sin, scale, o_ref):
    lane = lax.broadcasted_iota(jnp.int32, (1, LANES), 1)
    first_half = (lane % QK_DIM) < (QK_DIM // 2)
    for c in range(SEG // LANES):
        a = acc[:, c * LANES:(c + 1) * LANES]
        partner = jnp.where(first_half, pltpu.roll(a, LANES - QK_DIM // 2, 1), pltpu.roll(a, QK_DIM // 2, 1))
        o_ref[:, c * LANES:(c + 1) * LANES] = ((a * cos + partner * sin) * scale).astype(BF16)


def _inproj_kernel(x_ref, w_ref, *rest, mode, first_scale):
    o_ref, wbf_ref = rest[-2:]
    j = pl.program_id(0)
    i = pl.program_id(1)

    def emit(ws):
        acc = jnp.dot(x_ref[...], ws[0], preferred_element_type=F32)
        scale = jnp.where(j == 0, first_scale, 1.0).astype(F32)
        if mode == "plain":
            o_ref[...] = (acc * scale).astype(BF16)
        elif mode == "rope":
            _rope_store(acc, rest[0][...], rest[1][...], scale, o_ref)
        else:
            o_ref[...] = jax.nn.sigmoid(acc + rest[0][...]).astype(BF16)

    _with_bf16_weights(i == 0, i != 0, [w_ref], [wbf_ref], emit, transpose=True)


PLAIN_SEGS = (0, 1, 2, 5, 6)
ROPE_SEGS = (3, 4)
GATE_SEGS = (7, 8, 9, 10, 11, 12)


def _inproj(x_bf, w_in_t, extras, extra_specs, layer, tm, mode, segs, first_scale):
    s, d = x_bf.shape
    jump = next((k for k in range(1, len(segs)) if segs[k] != segs[k - 1] + 1), len(segs))
    gap = segs[jump] - segs[jump - 1] - 1 if jump < len(segs) else 0
    seg_of = lambda j: segs[0] + j + jnp.where(j >= jump, gap, 0)
    return pl.pallas_call(
        functools.partial(_inproj_kernel, mode=mode, first_scale=first_scale),
        grid=(len(segs), s // tm),
        in_specs=[pl.BlockSpec((tm, d), lambda j, i: (i, 0)),
                  pl.BlockSpec((None, SEG, d), lambda j, i: (layer, seg_of(j), 0))] + extra_specs,
        out_specs=pl.BlockSpec((tm, SEG), lambda j, i: (i, j)),
        out_shape=jax.ShapeDtypeStruct((s, len(segs) * SEG), BF16),
        scratch_shapes=[pltpu.VMEM((d, SEG), BF16)],
        compiler_params=_cparams(("arbitrary", "arbitrary"), 48),
        name="inproj_" + mode,
    )(x_bf, w_in_t, *extras)


def _inproj_all(x_bf, w_in_t, b_gate_flat, cos_t, sin_t, layer, tm):
    table_spec = pl.BlockSpec((tm, LANES), lambda j, i: (i, 0))
    plain = _inproj(x_bf, w_in_t, [], [], layer, tm, "plain", PLAIN_SEGS, FOX_Q_SCALE)
    rope = _inproj(x_bf, w_in_t, [cos_t, sin_t], [table_spec, table_spec], layer, tm, "rope", ROPE_SEGS,
                   DIFF_Q_SCALE)
    gates = _inproj(x_bf, w_in_t, [b_gate_flat], [pl.BlockSpec((None, 1, SEG), lambda j, i: (layer, 0, j))],
                    layer, tm, "gate", GATE_SEGS, 1.0)
    return plain, rope, gates


def _split3(v):
    hi = v.astype(BF16)
    r = v - hi.astype(F32)
    mid = r.astype(BF16)
    lo = (r - mid.astype(F32)).astype(BF16)
    return hi, mid, lo


def _forget_prep_kernel(x_ref, wfg_ref, bf_ref, fq_ref, fk_ref, qa_ref, ka_ref, carry_ref, *, tm):
    i = pl.program_id(0)

    @pl.when(i == 0)
    def _init():
        carry_ref[...] = jnp.zeros_like(carry_ref)

    w_fg = jnp.concatenate([wfg_ref[...], jnp.zeros((LANES - N_HEADS, wfg_ref.shape[1]), F32)], axis=0)
    z = lax.dot_general(x_ref[...], w_fg.astype(BF16), (((1,), (1,)), ((), ())),
                        preferred_element_type=F32) + bf_ref[...]
    log_f = -(jnp.maximum(-z, 0.0) + jnp.log1p(jnp.exp(-jnp.abs(z)))) * LOG2E
    row = lax.broadcasted_iota(jnp.int32, (tm, tm), 0)
    col = lax.broadcasted_iota(jnp.int32, (tm, tm), 1)
    tri = (col <= row).astype(BF16)
    hi, mid, lo = _split3(log_f)
    cum = (jnp.dot(tri, hi, preferred_element_type=F32) + jnp.dot(tri, mid, preferred_element_type=F32)
           + jnp.dot(tri, lo, preferred_element_type=F32)) + carry_ref[...]
    carry_ref[...] = cum[tm - 1:tm, :]
    c_hi, c_mid, c_lo = (p.astype(F32) for p in _split3(cum))
    lane = lax.broadcasted_iota(jnp.int32, (1, LANES), 1)
    for h in range(N_HEADS):
        ch, cm, cl = c_hi[:, h:h + 1], c_mid[:, h:h + 1], c_lo[:, h:h + 1]
        q_extra = jnp.where(lane == 0, ch, jnp.where(lane == 1, cm, jnp.where(lane == 2, cl,
                  jnp.where(lane < 6, 1.0, 0.0))))
        k_extra = jnp.where(lane < 3, 1.0, jnp.where(lane == 3, -ch, jnp.where(lane == 4, -cm,
                  jnp.where(lane == 5, -cl, 0.0))))
        q_aug = jnp.concatenate([fq_ref[:, h * HEAD_DIM:(h + 1) * HEAD_DIM].astype(F32), q_extra], axis=1)
        qa_ref[h] = q_aug.T.astype(BF16)
        ka_ref[h, :, 0:HEAD_DIM] = fk_ref[:, h * HEAD_DIM:(h + 1) * HEAD_DIM]
        ka_ref[h, :, HEAD_DIM:2 * HEAD_DIM] = k_extra.astype(BF16)


def _forget_prep(x_bf, w_in_t, b_fg, proj, layer, tm, tq):
    s, d = x_bf.shape
    per_tile = tq // tm
    return pl.pallas_call(
        functools.partial(_forget_prep_kernel, tm=tm),
        grid=(s // tm,),
        in_specs=[
            pl.BlockSpec((tm, d), lambda i: (i, 0)),
            pl.BlockSpec((None, N_HEADS, d), lambda i: (layer, N_MAIN // N_HEADS, 0)),
            pl.BlockSpec((1, LANES), lambda i: (0, 0)),
            pl.BlockSpec((tm, SEG), lambda i: (i, 0)),
            pl.BlockSpec((tm, SEG), lambda i: (i, 1)),
        ],
        out_specs=[pl.BlockSpec((N_HEADS, None, 2 * HEAD_DIM, tm),
                                lambda i: (0, i // per_tile, 0, i % per_tile)),
                   pl.BlockSpec((N_HEADS, tm, 2 * HEAD_DIM), lambda i: (0, i, 0))],
        out_shape=[jax.ShapeDtypeStruct((N_HEADS, s // tq, 2 * HEAD_DIM, tq), BF16),
                   jax.ShapeDtypeStruct((N_HEADS, s, 2 * HEAD_DIM), BF16)],
        scratch_shapes=[pltpu.VMEM((1, LANES), F32)],
        compiler_params=_cparams(("arbitrary",), 32),
        name="forget_prep",
    )(x_bf, w_in_t, b_fg, proj, proj)


def _transpose_kernel(fox_v_ref, diff_q_ref, diff_v_ref, o_ref, *, tk):
    for g, ref in enumerate((fox_v_ref, diff_q_ref, diff_v_ref)):
        xt = ref[...].astype(F32).T
        for c in range(o_ref.shape[1]):
            o_ref[g, c] = xt[:, c * tk:(c + 1) * tk].astype(BF16)


def _transpose_segs(plain, rope, tm, tk):
    s = plain.shape[0]
    seg_spec = lambda blk: pl.BlockSpec((tm, SEG), lambda i: (i, blk))
    return pl.pallas_call(
        functools.partial(_transpose_kernel, tk=tk),
        grid=(s // tm,),
        in_specs=[seg_spec(PLAIN_SEGS.index(2)), seg_spec(ROPE_SEGS.index(3)), seg_spec(PLAIN_SEGS.index(5))],
        out_specs=pl.BlockSpec((3, tm // tk, SEG, tk), lambda i: (0, i, 0, 0)),
        out_shape=jax.ShapeDtypeStruct((3, s // tk, SEG, tk), BF16),
        compiler_params=_cparams(("arbitrary",), 40),
        name="transpose_segs",
    )(plain, rope, plain)


ONES_ROWS = 16


class _AttnState:
    S_SLOTS = 4
    P_SLOTS = 2
    N_REFS = S_SLOTS + 2 * P_SLOTS + 2

    def __init__(self, refs):
        a, b = self.S_SLOTS, self.S_SLOTS + self.P_SLOTS
        self.s = refs[0:a]
        self.p = refs[a:b]
        self.alpha = refs[b:b + self.P_SLOTS]
        self.m, self.acc = refs[b + self.P_SLOTS:]

    @classmethod
    def scratch(cls, tq, tk):
        row = pltpu.VMEM((1, tq), F32)
        return ([pltpu.VMEM((tk, tq), F32)] * cls.S_SLOTS + [pltpu.VMEM((tk, tq), BF16)] * cls.P_SLOTS
                + [row] * cls.P_SLOTS + [row, pltpu.VMEM((HEAD_DIM + ONES_ROWS, tq), F32)])

    def init(self):
        self.m[...] = jnp.full_like(self.m, -jnp.inf)
        self.acc[...] = jnp.zeros_like(self.acc)
        self.p[1][...] = jnp.zeros_like(self.p[1])
        self.alpha[1][...] = jnp.ones_like(self.alpha[1])

    def normalized(self):
        return self.acc[0:HEAD_DIM, :] / self.acc[HEAD_DIM:HEAD_DIM + 1, :]


def _attn_pipeline(i, q_t, k_ref, v_t_ref, states, tq, tk):
    n_sub = tq // tk
    assert tq == n_sub * tk and n_sub == _AttnState.S_SLOTS

    def scores(c, slot, q0=0):
        k = k_ref[pl.ds(pl.multiple_of(c * tk, tk), tk), :]
        for st, q in zip(states, q_t):
            st.s[slot][:, q0:] = jnp.dot(k, q[:, q0:], preferred_element_type=F32)

    def softmax(s_slot, slot, k0=None, q0=0):
        for st in states:
            s = st.s[s_slot][:, q0:]
            if k0 is not None:
                key = lax.broadcasted_iota(jnp.int32, s.shape, 0) + k0
                query = lax.broadcasted_iota(jnp.int32, s.shape, 1) + (i * tq + q0)
                s = jnp.where(key <= query, s, -jnp.inf)
            m_prev = st.m[:, q0:]
            m_new = jnp.maximum(m_prev, jnp.max(s, axis=0, keepdims=True))
            st.m[:, q0:] = m_new
            st.alpha[slot][:, q0:] = jnp.exp2(m_prev - m_new)
            st.p[slot][:, q0:] = jnp.exp2(s - m_new).astype(BF16)

    def values(c, slot, q0=0):
        v_t = jnp.concatenate([v_t_ref[c], jnp.ones((ONES_ROWS, tk), BF16)], axis=0)
        for st in states:
            st.acc[:, q0:] = st.alpha[slot][:, q0:] * st.acc[:, q0:] + jnp.dot(
                v_t, st.p[slot][:, q0:], preferred_element_type=F32)

    for st in states:
        st.init()
    scores(0, 0)
    scores(1, 1)

    def body(block, carry):
        c = n_sub * block
        for u in range(n_sub):
            scores(c + u + 2, (u + 2) % n_sub)
            softmax(u, u % 2)
            values(jnp.maximum(c + u - 1, 0), (u + 1) % 2)
        return carry

    lax.fori_loop(0, i, body, 0)
    c = n_sub * i
    for u in range(n_sub):
        if u + 2 < n_sub:
            scores(c + u + 2, u + 2, q0=(u + 2) * tk)
        softmax(u, u % 2, k0=(c + u) * tk, q0=u * tk)
        values(jnp.maximum(c + u - 1, 0), (u + 1) % 2, q0=max(u - 1, 0) * tk)
    values(c + n_sub - 1, (n_sub - 1) % 2, q0=(n_sub - 1) * tk)


def _fox_attn_kernel(q_ref, k_ref, v_ref, o_ref, *scratch, tq, tk):
    st = _AttnState(scratch)
    _attn_pipeline(pl.program_id(1), [q_ref[...]], k_ref, v_ref, [st], tq, tk)
    o_ref[...] = st.normalized().T.astype(BF16)


def _fox_attn(q_aug_t, k_aug, segs_t, tq, tk):
    _, s, _ = k_aug.shape
    return pl.pallas_call(
        functools.partial(_fox_attn_kernel, tq=tq, tk=tk),
        grid=(N_HEADS, s // tq),
        in_specs=[
            pl.BlockSpec((None, None, 2 * HEAD_DIM, tq), lambda h, i: (h, i, 0, 0)),
            pl.BlockSpec((None, s, 2 * HEAD_DIM), lambda h, i: (h, 0, 0)),
            pl.BlockSpec((None, s // tk, HEAD_DIM, tk), lambda h, i: (0, 0, h, 0)),
        ],
        out_specs=pl.BlockSpec((tq, HEAD_DIM), lambda h, i: (i, h)),
        out_shape=jax.ShapeDtypeStruct((s, SEG), BF16),
        scratch_shapes=_AttnState.scratch(tq, tk),
        compiler_params=_cparams(("arbitrary", "arbitrary"), 40),
        name="fox_attn",
    )(q_aug_t, k_aug, segs_t)


def _diff_attn_kernel(q_ref, k_ref, v_ref, dl_ref, gain_ref, o_ref, *scratch, tq, tk, lambda_init):
    st1 = _AttnState(scratch[:_AttnState.N_REFS])
    st2 = _AttnState(scratch[_AttnState.N_REFS:])
    q_t = jnp.concatenate([q_ref[c] for c in range(tq // tk)], axis=1)
    channel = lax.broadcasted_iota(jnp.int32, (HEAD_DIM, 1), 0)
    q1 = jnp.where(channel < QK_DIM, q_t, jnp.zeros_like(q_t))
    q2 = jnp.where(channel >= QK_DIM, q_t, jnp.zeros_like(q_t))
    _attn_pipeline(pl.program_id(1), [q1, q2], k_ref, v_ref, [st1, st2], tq, tk)

    dl = dl_ref[...]
    lam = (jnp.exp(jnp.sum(dl[0:1, :] * dl[1:2, :], axis=1, keepdims=True))
           - jnp.exp(jnp.sum(dl[2:3, :] * dl[3:4, :], axis=1, keepdims=True)) + lambda_init)
    o = st1.normalized() - lam * st2.normalized()
    o = o * lax.rsqrt(jnp.mean(o * o, axis=0, keepdims=True) + LN_EPS)
    o_ref[...] = (o.T * gain_ref[...] * (1.0 - lambda_init)).astype(BF16)


def _diff_attn(rope, segs_t, diff_lambda, diff_gain, layer, lambda_init, tq, tk):
    s = rope.shape[0]
    k_block0 = ROPE_SEGS.index(4) * SEG // HEAD_DIM
    return pl.pallas_call(
        functools.partial(_diff_attn_kernel, tq=tq, tk=tk, lambda_init=lambda_init),
        grid=(N_HEADS, s // tq),
        in_specs=[
            pl.BlockSpec((None, tq // tk, HEAD_DIM, tk), lambda h, i: (1, i, h, 0)),
            pl.BlockSpec((s, HEAD_DIM), lambda h, i: (0, k_block0 + h)),
            pl.BlockSpec((None, s // tk, HEAD_DIM, tk), lambda h, i: (2, 0, h, 0)),
            pl.BlockSpec((None, 4, QK_DIM), lambda h, i: (layer, 0, 0)),
            pl.BlockSpec((None, 1, HEAD_DIM), lambda h, i: (layer, 0, 0)),
        ],
        out_specs=pl.BlockSpec((tq, HEAD_DIM), lambda h, i: (i, h)),
        out_shape=jax.ShapeDtypeStruct((s, SEG), BF16),
        scratch_shapes=_AttnState.scratch(tq, tk) * 2,
        compiler_params=_cparams(("arbitrary", "arbitrary"), 40),
        name="diff_attn",
    )(segs_t, rope, segs_t, diff_lambda, diff_gain)


def _pool_kernel(u_ref, halo_ref, wg_ref, sc_ref, o_ref, *, tm):
    i = pl.program_id(0)
    tokens_seen = (lax.broadcasted_iota(jnp.int32, (tm, 1), 0) + i * tm + 1).astype(F32)
    for g, window in enumerate(POOL_WINDOWS):
        cols = slice(g * POOL_GROUP_DIM, (g + 1) * POOL_GROUP_DIM)
        u = u_ref[:, cols].astype(F32)
        halo = halo_ref[:, cols].astype(F32)
        halo = jnp.where(i > 0, halo, jnp.zeros_like(halo))
        ext = jnp.concatenate([halo, u], axis=0)
        shift = 1
        while shift < window:
            ext = ext + pltpu.roll(ext, shift, 0)
            shift *= 2
        pooled = ext[POOL_HALO:, :] / jnp.minimum(tokens_seen, float(window))
        delta = (pooled - u).astype(BF16)
        y = jnp.dot(delta, wg_ref[g].astype(BF16), preferred_element_type=F32) * sc_ref[:, cols]
        o_ref[:, cols] = y.astype(BF16)


def _pool(plain, w_pool_group, pool_scale, layer, tm):
    s = plain.shape[0]
    n_groups = len(POOL_WINDOWS)
    halo_blocks_per_tile = tm // POOL_HALO
    blk = PLAIN_SEGS.index(6)
    return pl.pallas_call(
        functools.partial(_pool_kernel, tm=tm),
        grid=(s // tm,),
        in_specs=[
            pl.BlockSpec((tm, SEG), lambda i: (i, blk)),
            pl.BlockSpec((POOL_HALO, SEG), lambda i: (jnp.maximum(i * halo_blocks_per_tile - 1, 0), blk)),
            pl.BlockSpec((None, n_groups, POOL_GROUP_DIM, POOL_GROUP_DIM), lambda i: (layer, 0, 0, 0)),
            pl.BlockSpec((None, 1, SEG), lambda i: (layer, 0, 0)),
        ],
        out_specs=pl.BlockSpec((tm, SEG), lambda i: (i, 0)),
        out_shape=jax.ShapeDtypeStruct((s, SEG), BF16),
        compiler_params=_cparams(("arbitrary",), 32),
        name="pool",
    )(plain, plain, w_pool_group, pool_scale)


def _merge_kernel(ya_ref, yb_ref, yc_ref, wa_ref, wb_ref, wc_ref, ga_ref, gb_ref, gc_ref, o_ref,
                  wab_ref, wbb_ref, wcb_ref):
    i = pl.program_id(1)

    def emit(ws):
        h = ga_ref[...].astype(F32) * jnp.dot(ya_ref[...], ws[0], preferred_element_type=F32)
        h = h + gb_ref[...].astype(F32) * jnp.dot(yb_ref[...], ws[1], preferred_element_type=F32)
        h = h + gc_ref[...].astype(F32) * jnp.dot(yc_ref[...], ws[2], preferred_element_type=F32)
        o_ref[...] = h.astype(BF16)

    _with_bf16_weights(i == 0, i != 0, [wa_ref, wb_ref, wc_ref], [wab_ref, wbb_ref, wcb_ref], emit)


def _merge(y_a, y_b, y_c, w_a, w_b, w_c, gates, layer, tm, tn):
    s = y_a.shape[0]
    gate_blocks = D_MODEL // tn
    y_spec = pl.BlockSpec((tm, SEG), lambda j, i: (i, 0))
    w_spec = pl.BlockSpec((None, SEG, tn), lambda j, i: (layer, 0, j))
    gate_spec = lambda br: pl.BlockSpec((tm, tn), lambda j, i: (i, br * gate_blocks + j))
    return pl.pallas_call(
        _merge_kernel,
        grid=(D_MODEL // tn, s // tm),
        in_specs=[y_spec, y_spec, y_spec, w_spec, w_spec, w_spec, gate_spec(0), gate_spec(1), gate_spec(2)],
        out_specs=pl.BlockSpec((tm, tn), lambda j, i: (i, j)),
        out_shape=jax.ShapeDtypeStruct((s, D_MODEL), BF16),
        scratch_shapes=[pltpu.VMEM((SEG, tn), BF16)] * 3,
        compiler_params=_cparams(("arbitrary", "arbitrary"), 48),
        name="merge",
    )(y_a, y_b, y_c, w_a, w_b, w_c, gates, gates, gates)


def _matmul_kernel(x_ref, w_ref, o_ref, wbf_ref):
    i = pl.program_id(1)

    def emit(ws):
        o_ref[...] = jnp.dot(x_ref[...], ws[0], preferred_element_type=F32).astype(o_ref.dtype)

    _with_bf16_weights(i == 0, i != 0, [w_ref], [wbf_ref], emit)


def _matmul(x_bf, w, layer, tm, tn, out_dtype):
    s, k = x_bf.shape
    n = w.shape[-1]
    return pl.pallas_call(
        _matmul_kernel,
        grid=(n // tn, s // tm),
        in_specs=[pl.BlockSpec((tm, k), lambda j, i: (i, 0)),
                  pl.BlockSpec((None, k, tn), lambda j, i: (layer, 0, j))],
        out_specs=pl.BlockSpec((tm, tn), lambda j, i: (i, j)),
        out_shape=jax.ShapeDtypeStruct((s, n), out_dtype),
        scratch_shapes=[pltpu.VMEM((k, tn), BF16)],
        compiler_params=_cparams(("arbitrary", "arbitrary"), 48),
        name="matmul",
    )(x_bf, w)


def _layer_norm_rows(z, g, b):
    mu = jnp.mean(z, axis=1, keepdims=True)
    zc = z - mu
    var = jnp.mean(zc * zc, axis=1, keepdims=True)
    return zc * lax.rsqrt(var + LN_EPS) * g + b


def _add_ln_kernel(x_ref, f_ref, g_ref, b_ref, xo_ref, xb_ref, *, alpha):
    y = _layer_norm_rows(alpha * x_ref[...] + f_ref[...], g_ref[...], b_ref[...])
    xo_ref[...] = y
    xb_ref[...] = y.astype(BF16)


def _add_ln(x, f, g, b, layer, alpha, tm):
    s, d = x.shape
    row_spec = pl.BlockSpec((tm, d), lambda i: (i, 0))
    par_spec = pl.BlockSpec((None, 1, d), lambda i: (layer, 0, 0))
    return pl.pallas_call(
        functools.partial(_add_ln_kernel, alpha=alpha),
        grid=(s // tm,),
        in_specs=[row_spec, row_spec, par_spec, par_spec],
        out_specs=[row_spec, row_spec],
        out_shape=[jax.ShapeDtypeStruct((s, d), F32), jax.ShapeDtypeStruct((s, d), BF16)],
        compiler_params=_cparams(("arbitrary",), 32),
        name="add_ln",
    )(x, f, g, b)


def _gate_up_kernel(te_ref, tf_ref, nu_ref, x_ref, wg_ref, wu_ref, o_ref, wgb_ref, wub_ref):
    t = pl.program_id(1)

    active = t < nu_ref[0]
    first = tf_ref[t] == 1

    def emit(ws):
        x = x_ref[...]
        g = jnp.dot(x, ws[0], preferred_element_type=F32)
        u = jnp.dot(x, ws[1], preferred_element_type=F32)
        o_ref[...] = (g * jax.nn.sigmoid(g) * u).astype(BF16)

    _with_bf16_weights(active & first, active & jnp.logical_not(first), [wg_ref, wu_ref], [wgb_ref, wub_ref], emit)

    @pl.when(t >= nu_ref[0])
    def _unused_tile():
        o_ref[...] = jnp.zeros_like(o_ref)


def _gate_up(groups, x_sorted, w_gate, w_up, layer, tm, tn):
    tile_expert, tile_first, n_used = groups
    n_slots, k = x_sorted.shape
    n = w_gate.shape[-1]
    row = lambda j, t, te, tf, nu: (jnp.minimum(t, nu[0] - 1), 0)
    w_spec = pl.BlockSpec((None, None, k, tn), lambda j, t, te, tf, nu: (layer, te[t], 0, j))
    return pl.pallas_call(
        _gate_up_kernel,
        grid_spec=pltpu.PrefetchScalarGridSpec(
            num_scalar_prefetch=3,
            grid=(n // tn, n_slots // tm),
            in_specs=[pl.BlockSpec((tm, k), row), w_spec, w_spec],
            out_specs=pl.BlockSpec((tm, tn), lambda j, t, te, tf, nu: (t, j)),
            scratch_shapes=[pltpu.VMEM((k, tn), BF16)] * 2,
        ),
        out_shape=jax.ShapeDtypeStruct((n_slots, n), BF16),
        compiler_params=_cparams(("arbitrary", "arbitrary"), 48),
        name="ffn_gate_up",
    )(tile_expert, tile_first, n_used, x_sorted, w_gate, w_up)


def _down_kernel(te_ref, tf_ref, nu_ref, h_ref, w_ref, o_ref, wbf_ref):
    t = pl.program_id(1)

    active = t < nu_ref[0]
    first = tf_ref[t] == 1

    def emit(ws):
        o_ref[...] = jnp.dot(h_ref[...], ws[0], preferred_element_type=F32)

    _with_bf16_weights(active & first, active & jnp.logical_not(first), [w_ref], [wbf_ref], emit)

    @pl.when(t >= nu_ref[0])
    def _unused_tile():
        o_ref[...] = jnp.zeros_like(o_ref)


def _down(groups, h_sorted, w_down, layer, tm, tn):
    tile_expert, tile_first, n_used = groups
    n_slots, k = h_sorted.shape
    n = w_down.shape[-1]
    return pl.pallas_call(
        _down_kernel,
        grid_spec=pltpu.PrefetchScalarGridSpec(
            num_scalar_prefetch=3,
            grid=(n // tn, n_slots // tm),
            in_specs=[pl.BlockSpec((tm, k), lambda j, t, te, tf, nu: (jnp.minimum(t, nu[0] - 1), 0)),
                      pl.BlockSpec((None, None, k, tn), lambda j, t, te, tf, nu: (layer, te[t], 0, j))],
            out_specs=pl.BlockSpec((tm, tn), lambda j, t, te, tf, nu: (t, j)),
            scratch_shapes=[pltpu.VMEM((k, tn), BF16)],
        ),
        out_shape=jax.ShapeDtypeStruct((n_slots, n), F32),
        compiler_params=_cparams(("arbitrary", "arbitrary"), 52),
        name="ffn_down",
    )(tile_expert, tile_first, n_used, h_sorted, w_down)


def _single_group(n_tiles):
    return (jnp.zeros((n_tiles,), jnp.int32),
            jnp.zeros((n_tiles,), jnp.int32).at[0].set(1),
            jnp.full((1,), n_tiles, jnp.int32))


def _router_kernel(x_ref, rw_ref, rb_ref, o_ref, cnt_ref, carry_ref, *, tm):
    i = pl.program_id(0)

    @pl.when(i == 0)
    def _init():
        carry_ref[...] = jnp.zeros_like(carry_ref)

    logits = jnp.dot(x_ref[...], rw_ref[...], preferred_element_type=F32,
                     precision=lax.Precision.HIGHEST) + rb_ref[...]
    lane = lax.broadcasted_iota(jnp.int32, (tm, LANES), 1).astype(F32)
    lg = jnp.where(lane < N_EXPERTS, logits, -jnp.inf)
    v1 = jnp.max(lg, axis=1, keepdims=True)
    e1 = jnp.min(jnp.where(lg == v1, lane, float(LANES)), axis=1, keepdims=True)
    lg2 = jnp.where(lane == e1, -jnp.inf, lg)
    v2 = jnp.max(lg2, axis=1, keepdims=True)
    e2 = jnp.min(jnp.where(lg2 == v2, lane, float(LANES)), axis=1, keepdims=True)
    t2 = jnp.exp(v2 - v1)
    w1 = 1.0 / (1.0 + t2)
    w2 = t2 / (1.0 + t2)
    pick1 = lane == e1
    pick2 = lane == e2
    onehot = jnp.where(pick1 | pick2, 1.0, 0.0)
    row = lax.broadcasted_iota(jnp.int32, (tm, tm), 0)
    col = lax.broadcasted_iota(jnp.int32, (tm, tm), 1)
    before = (col < row).astype(BF16)
    seen = jnp.dot(before, onehot.astype(BF16), preferred_element_type=F32) + carry_ref[...]
    r1 = jnp.sum(jnp.where(pick1, seen, 0.0), axis=1, keepdims=True)
    r2 = jnp.sum(jnp.where(pick2, seen, 0.0), axis=1, keepdims=True)
    carry_ref[...] = carry_ref[...] + jnp.sum(onehot, axis=0, keepdims=True)
    o_ref[...] = jnp.where(lane == 0, e1, jnp.where(lane == 1, e2, jnp.where(lane == 2, w1,
                 jnp.where(lane == 3, w2, jnp.where(lane == 4, r1, jnp.where(lane == 5, r2, 0.0))))))
    cnt_ref[...] = carry_ref[...]


def _router(x, rw_pad, rb_pad, tm):
    s, d = x.shape
    return pl.pallas_call(
        functools.partial(_router_kernel, tm=tm),
        grid=(s // tm,),
        in_specs=[pl.BlockSpec((tm, d), lambda i: (i, 0)),
                  pl.BlockSpec((d, LANES), lambda i: (0, 0)),
                  pl.BlockSpec((1, LANES), lambda i: (0, 0))],
        out_specs=[pl.BlockSpec((tm, LANES), lambda i: (i, 0)), pl.BlockSpec((1, LANES), lambda i: (0, 0))],
        out_shape=[jax.ShapeDtypeStruct((s, LANES), F32), jax.ShapeDtypeStruct((1, LANES), F32)],
        scratch_shapes=[pltpu.VMEM((1, LANES), F32)],
        compiler_params=_cparams(("arbitrary",), 32),
        name="router",
    )(x, rw_pad, rb_pad)


def _scatter_copies(x_ref, o_hbm, p1_ref, p2_ref, base, r, sem):
    return (pltpu.make_async_copy(x_ref.at[r], o_hbm.at[p1_ref[base + r]], sem),
            pltpu.make_async_copy(x_ref.at[r], o_hbm.at[p2_ref[base + r]], sem))


def _dispatch_kernel(p1_ref, p2_ref, x_ref, init_hbm, o_hbm, sem, *, tt):
    del init_hbm
    base = pl.program_id(0) * tt

    def issue(r, carry):
        for cp in _scatter_copies(x_ref, o_hbm, p1_ref, p2_ref, base, r, sem):
            cp.start()
        return carry

    lax.fori_loop(0, tt, issue, 0, unroll=8)

    def drain(r, carry):
        for cp in _scatter_copies(x_ref, o_hbm, p1_ref, p2_ref, base, r, sem):
            cp.wait()
        return carry

    lax.fori_loop(0, tt, drain, 0, unroll=8)


def _dispatch(pos1, pos2, x_bf, n_slots, tt):
    s, d = x_bf.shape
    x_slabs = x_bf.reshape(s, d // LANES, LANES)
    out = pl.pallas_call(
        functools.partial(_dispatch_kernel, tt=tt),
        grid_spec=pltpu.PrefetchScalarGridSpec(
            num_scalar_prefetch=2,
            grid=(s // tt,),
            in_specs=[pl.BlockSpec((tt, d // LANES, LANES), lambda i, p1, p2: (i, 0, 0)),
                      pl.BlockSpec(memory_space=pl.ANY)],
            out_specs=pl.BlockSpec(memory_space=pl.ANY),
            scratch_shapes=[pltpu.SemaphoreType.DMA],
        ),
        out_shape=jax.ShapeDtypeStruct((n_slots, d // LANES, LANES), BF16),
        input_output_aliases={3: 0},
        compiler_params=_cparams(("arbitrary",), 32),
        name="moe_dispatch",
    )(pos1, pos2, x_slabs, jnp.zeros((n_slots, d // LANES, LANES), BF16))
    return out.reshape(n_slots, d)


def _row_copy(src_hbm, src_row, buf, dst_row, sem):
    return pltpu.make_async_copy(src_hbm.at[pl.ds(src_row, 1), :], buf.at[pl.ds(dst_row, 1), :], sem)


def _combine_ln_kernel(p1_ref, p2_ref, y_hbm, x_ref, route_ref, g_ref, b_ref, xo_ref, xb_ref,
                       buf, sem, *, tc, alpha):
    i = pl.program_id(0)

    def copies(step, slot, r):
        t = step * tc + r
        return (_row_copy(y_hbm, p1_ref[t], buf.at[slot, 0], r, sem.at[slot]),
                _row_copy(y_hbm, p2_ref[t], buf.at[slot, 1], r, sem.at[slot]))

    def issue(step, slot):
        def body(r, carry):
            for cp in copies(step, slot, r):
                cp.start()
            return carry

        lax.fori_loop(0, tc, body, 0, unroll=8)

    @pl.when(i == 0)
    def _first():
        issue(0, 0)

    @pl.when(i + 1 < pl.num_programs(0))
    def _prefetch():
        issue(i + 1, (i + 1) % 2)

    slot = i % 2

    def drain(r, carry):
        for cp in copies(i, slot, r):
            cp.wait()
        return carry

    lax.fori_loop(0, tc, drain, 0, unroll=8)
    route = route_ref[...]
    f = route[:, 2:3] * buf[slot, 0] + route[:, 3:4] * buf[slot, 1]
    y = _layer_norm_rows(alpha * x_ref[...] + f, g_ref[...], b_ref[...])
    xo_ref[...] = y
    xb_ref[...] = y.astype(BF16)


def _combine_ln(pos1, pos2, y_sorted, x, route, g, b, layer, alpha, tc):
    s, d = x.shape
    row_spec = pl.BlockSpec((tc, d), lambda i, p1, p2: (i, 0))
    par_spec = pl.BlockSpec((None, 1, d), lambda i, p1, p2: (layer, 0, 0))
    return pl.pallas_call(
        functools.partial(_combine_ln_kernel, tc=tc, alpha=alpha),
        grid_spec=pltpu.PrefetchScalarGridSpec(
            num_scalar_prefetch=2,
            grid=(s // tc,),
            in_specs=[pl.BlockSpec(memory_space=pl.ANY), row_spec,
                      pl.BlockSpec((tc, LANES), lambda i, p1, p2: (i, 0)), par_spec, par_spec],
            out_specs=[row_spec, row_spec],
            scratch_shapes=[pltpu.VMEM((2, 2, tc, d), F32), pltpu.SemaphoreType.DMA((2,))],
        ),
        out_shape=[jax.ShapeDtypeStruct((s, d), F32), jax.ShapeDtypeStruct((s, d), BF16)],
        compiler_params=_cparams(("arbitrary",), 32),
        name="moe_combine_ln",
    )(pos1, pos2, y_sorted, x, route, g, b)


def _moe_plan(route, counts, tm, n_tiles):
    expert = route[:, 0:2].astype(jnp.int32)
    rank = route[:, 4:6].astype(jnp.int32)
    count = counts[0, :N_EXPERTS].astype(jnp.int32)
    padded = ((count + tm - 1) // tm) * tm
    ends = jnp.cumsum(padded)
    starts = ends - padded
    pos = starts[expert] + rank
    n_used = ends[-1] // tm
    tile = jnp.arange(n_tiles, dtype=jnp.int32)
    last = jnp.minimum(tile, n_used - 1)
    tile_expert = jnp.sum(last[:, None] * tm >= ends[None, :], axis=1).astype(jnp.int32)
    prev = jnp.concatenate([jnp.full((1,), -1, jnp.int32), tile_expert[:-1]])
    tile_first = (tile_expert != prev).astype(jnp.int32)
    return pos, (tile_expert, tile_first, n_used.reshape(1).astype(jnp.int32))


def _rope_tables(seq):
    pos = jnp.arange(seq, dtype=F32)
    inv = ROPE_THETA ** (-jnp.arange(0, QK_DIM, 2, dtype=F32) / QK_DIM)
    ang = pos[:, None] * inv[None, :]
    cos, sin = jnp.cos(ang), jnp.sin(ang)
    return jnp.tile(cos, (1, 4)), jnp.tile(jnp.concatenate([-sin, sin], axis=1), (1, 2))


def _pad_lanes(a):
    return jnp.pad(a, ((0, 0), (0, LANES - a.shape[1])))


def kernel(x, w_in, b_forget, w_pool_group, pool_scale, diff_lambda, diff_norm_gain, w_branch_a, w_branch_b, w_branch_c, b_gate, w_out, ln1_g, ln1_b, ln2_g, ln2_b, ffn_w_gate, ffn_w_up, ffn_w_down, router_w, router_b, expert_w_gate, expert_w_up, expert_w_down):
    batch, s, d = x.shape
    assert batch == 1 and d == D_MODEL and w_in.shape[-1] == N_MAIN + N_HEADS
    depth = w_in.shape[0]
    alpha = (2 * depth) ** 0.25

    tm = min(1024, s)
    tq = min(1024, s)
    tk = tq // 4
    te = min(512, s)
    tg = min(256, s)
    n_ff = 512
    n_tiles = 2 * s // te + N_EXPERTS
    n_slots = n_tiles * te

    cos_t, sin_t = _rope_tables(s)
    w_in_t = jnp.swapaxes(w_in, 1, 2)
    b_gate_flat = b_gate.reshape(depth, 1, 3 * D_MODEL)
    pool_scale3 = pool_scale.reshape(depth, 1, SEG)
    gain3 = diff_norm_gain.reshape(depth, 1, HEAD_DIM)
    ln = [p.reshape(depth, 1, D_MODEL) for p in (ln1_g, ln1_b, ln2_g, ln2_b)]
    dense_w = [w.reshape(w.shape[0], 1, *w.shape[1:]) for w in (ffn_w_gate, ffn_w_up, ffn_w_down)]

    xf = x.reshape(s, d)
    xb = xf.astype(BF16)
    for layer in range(depth):
        lambda_init = 0.8 - 0.6 * math.exp(-0.3 * layer)
        plain, rope, gates = _inproj_all(xb, w_in_t, b_gate_flat, cos_t, sin_t, layer, tm)
        b_fg = _pad_lanes(b_forget[layer].reshape(1, N_HEADS))
        q_aug_t, k_aug = _forget_prep(xb, w_in_t, b_fg, plain, layer, te, tq)
        segs_t = _transpose_segs(plain, rope, te, tk)
        y_a = _fox_attn(q_aug_t, k_aug, segs_t, tq, tk)
        y_b = _pool(plain, w_pool_group, pool_scale3, layer, tm)
        y_c = _diff_attn(rope, segs_t, diff_lambda, gain3, layer, lambda_init, tq, tk)
        h = _merge(y_a, y_b, y_c, w_branch_a, w_branch_b, w_branch_c, gates, layer, tm, 512)
        mix = _matmul(h, w_out, layer, tm, SEG, F32)
        xf, xb = _add_ln(xf, mix, ln[0], ln[1], layer, alpha, tg)

        j = layer // 2
        if layer % 2 == 0:
            hidden = _gate_up(_single_group(s // tm), xb, dense_w[0], dense_w[1], j, tm, n_ff)
            f = _down(_single_group(s // te), hidden, dense_w[2], j, te, 512)
            xf, xb = _add_ln(xf, f, ln[2], ln[3], layer, alpha, tg)
        else:
            route, counts = _router(xf, _pad_lanes(router_w[j]), _pad_lanes(router_b[j].reshape(1, N_EXPERTS)), te)
            pos, groups = _moe_plan(route, counts, te, n_tiles)
            x_sorted = _dispatch(pos[:, 0], pos[:, 1], xb, n_slots, te)
            hidden = _gate_up(groups, x_sorted, expert_w_gate, expert_w_up, j, te, n_ff)
            y_sorted = _down(groups, hidden, expert_w_down, j, te, 512)
            xf, xb = _combine_ln(pos[:, 0], pos[:, 1], y_sorted, xf, route, ln[2], ln[3], layer, alpha, tg)
    return xf.reshape(batch, s, d)
```

```python
import functools
import math

import jax
import jax.numpy as jnp
from jax import lax
from jax.experimental import pallas as pl
from jax.experimental.pallas import tpu as pltpu

BF16 = jnp.bfloat16
F32 = jnp.float32

D_MODEL = 2048
N_HEADS = 8
HEAD_DIM = 128
QK_DIM = 64
SEG = 1024
N_MAIN = 13 * SEG
POOL_WINDOWS = (2, 4, 8, 16)
POOL_GROUP_DIM = 256
POOL_HALO = 16
N_EXPERTS = 8
ROPE_THETA = 10000.0
LN_EPS = 1e-5
LOG2E = math.log2(math.e)
FOX_Q_SCALE = HEAD_DIM ** -0.5 * LOG2E
DIFF_Q_SCALE = QK_DIM ** -0.5 * LOG2E
LANES = 128

MIB = 2 ** 20


def _cparams(semantics, vmem_mib):
    return pltpu.CompilerParams(dimension_semantics=semantics, vmem_limit_bytes=vmem_mib * MIB)


def _with_bf16_weights(fresh, cached, w_refs, wbf_refs, emit, transpose=False):
    @pl.when(fresh)
    def _fresh():
        ws = [(w[...].T if transpose else w[...]).astype(BF16) for w in w_refs]
        for dst, w in zip(wbf_refs, ws):
            dst[...] = w
        emit(ws)

    @pl.when(cached)
    def _cached():
        emit([r[...] for r in wbf_refs])


def _rope_store(acc, cos, sin, scale, o_ref):
    lane = lax.broadcasted_iota(jnp.int32, (1, LANES), 1)
    first_half = (lane % QK_DIM) < (QK_DIM // 2)
    for c in range(SEG // LANES):
        a = acc[:, c * LANES:(c + 1) * LANES]
        partner = jnp.where(first_half, pltpu.roll(a, LANES - QK_DIM // 2, 1), pltpu.roll(a, QK_DIM // 2, 1))
        o_ref[:, c * LANES:(c + 1) * LANES] = ((a * cos + partner * sin) * scale).astype(BF16)


def _inproj_kernel(x_ref, w_ref, *rest, mode, first_scale):
    o_ref, wbf_ref = rest[-2:]
    j = pl.program_id(0)
    i = pl.program_id(1)

    def emit(ws):
        acc = jnp.dot(x_ref[...], ws[0], preferred_element_type=F32)
        scale = jnp.where(j == 0, first_scale, 1.0).astype(F32)
        if mode == "plain":
            o_ref[...] = (acc * scale).astype(BF16)
        elif mode == "rope":
            _rope_store(acc, rest[0][...], rest[1][...], scale, o_ref)
        else:
            o_ref[...] = jax.nn.sigmoid(acc + rest[0][...]).astype(BF16)

    _with_bf16_weights(i == 0, i != 0, [w_ref], [wbf_ref], emit, transpose=True)


PLAIN_SEGS = (0, 1, 2, 5, 6)
ROPE_SEGS = (3, 4)
GATE_SEGS = (7, 8, 9, 10, 11, 12)


def _inproj(x_bf, w_in_t, extras, extra_specs, layer, tm, mode, segs, first_scale):
    s, d = x_bf.shape
    jump = next((k for k in range(1, len(segs)) if segs[k] != segs[k - 1] + 1), len(segs))
    gap = segs[jump] - segs[jump - 1] - 1 if jump < len(segs) else 0
    seg_of = lambda j: segs[0] + j + jnp.where(j >= jump, gap, 0)
    return pl.pallas_call(
        functools.partial(_inproj_kernel, mode=mode, first_scale=first_scale),
        grid=(len(segs), s // tm),
        in_specs=[pl.BlockSpec((tm, d), lambda j, i: (i, 0)),
                  pl.BlockSpec((None, SEG, d), lambda j, i: (layer, seg_of(j), 0))] + extra_specs,
        out_specs=pl.BlockSpec((tm, SEG), lambda j, i: (i, j)),
        out_shape=jax.ShapeDtypeStruct((s, len(segs) * SEG), BF16),
        scratch_shapes=[pltpu.VMEM((d, SEG), BF16)],
        compiler_params=_cparams(("arbitrary", "arbitrary"), 48),
        name="inproj_" + mode,
    )(x_bf, w_in_t, *extras)


def _inproj_all(x_bf, w_in_t, b_gate_flat, cos_t, sin_t, layer, tm):
    table_spec = pl.BlockSpec((tm, LANES), lambda j, i: (i, 0))
    plain = _inproj(x_bf, w_in_t, [], [], layer, tm, "plain", PLAIN_SEGS, FOX_Q_SCALE)
    rope = _inproj(x_bf, w_in_t, [cos_t, sin_t], [table_spec, table_spec], layer, tm, "rope", ROPE_SEGS,
                   DIFF_Q_SCALE)
    gates = _inproj(x_bf, w_in_t, [b_gate_flat], [pl.BlockSpec((None, 1, SEG), lambda j, i: (layer, 0, j))],
                    layer, tm, "gate", GATE_SEGS, 1.0)
    return plain, rope, gates


def _split3(v):
    hi = v.astype(BF16)
    r = v - hi.astype(F32)
    mid = r.astype(BF16)
    lo = (r - mid.astype(F32)).astype(BF16)
    return hi, mid, lo


def _forget_prep_kernel(x_ref, wfg_ref, bf_ref, fq_ref, fk_ref, qa_ref, ka_ref, carry_ref, *, tm):
    i = pl.program_id(0)

    @pl.when(i == 0)
    def _init():
        carry_ref[...] = jnp.zeros_like(carry_ref)

    w_fg = jnp.concatenate([wfg_ref[...], jnp.zeros((LANES - N_HEADS, wfg_ref.shape[1]), F32)], axis=0)
    z = lax.dot_general(x_ref[...], w_fg.astype(BF16), (((1,), (1,)), ((), ())),
                        preferred_element_type=F32) + bf_ref[...]
    log_f = -(jnp.maximum(-z, 0.0) + jnp.log1p(jnp.exp(-jnp.abs(z)))) * LOG2E
    row = lax.broadcasted_iota(jnp.int32, (tm, tm), 0)
    col = lax.broadcasted_iota(jnp.int32, (tm, tm), 1)
    tri = (col <= row).astype(BF16)
    hi, mid, lo = _split3(log_f)
    cum = (jnp.dot(tri, hi, preferred_element_type=F32) + jnp.dot(tri, mid, preferred_element_type=F32)
           + jnp.dot(tri, lo, preferred_element_type=F32)) + carry_ref[...]
    carry_ref[...] = cum[tm - 1:tm, :]
    c_hi, c_mid, c_lo = (p.astype(F32) for p in _split3(cum))
    lane = lax.broadcasted_iota(jnp.int32, (1, LANES), 1)
    for h in range(N_HEADS):
        ch, cm, cl = c_hi[:, h:h + 1], c_mid[:, h:h + 1], c_lo[:, h:h + 1]
        q_extra = jnp.where(lane == 0, ch, jnp.where(lane == 1, cm, jnp.where(lane == 2, cl,
                  jnp.where(lane < 6, 1.0, 0.0))))
        k_extra = jnp.where(lane < 3, 1.0, jnp.where(lane == 3, -ch, jnp.where(lane == 4, -cm,
                  jnp.where(lane == 5, -cl, 0.0))))
        q_aug = jnp.concatenate([fq_ref[:, h * HEAD_DIM:(h + 1) * HEAD_DIM].astype(F32), q_extra], axis=1)
        qa_ref[h] = q_aug.T.astype(BF16)
        ka_ref[h, :, 0:HEAD_DIM] = fk_ref[:, h * HEAD_DIM:(h + 1) * HEAD_DIM]
        ka_ref[h, :, HEAD_DIM:2 * HEAD_DIM] = k_extra.astype(BF16)


def _forget_prep(x_bf, w_in_t, b_fg, proj, layer, tm, tq):
    s, d = x_bf.shape
    per_tile = tq // tm
    return pl.pallas_call(
        functools.partial(_forget_prep_kernel, tm=tm),
        grid=(s // tm,),
        in_specs=[
            pl.BlockSpec((tm, d), lambda i: (i, 0)),
            pl.BlockSpec((None, N_HEADS, d), lambda i: (layer, N_MAIN // N_HEADS, 0)),
            pl.BlockSpec((1, LANES), lambda i: (0, 0)),
            pl.BlockSpec((tm, SEG), lambda i: (i, 0)),
            pl.BlockSpec((tm, SEG), lambda i: (i, 1)),
        ],
        out_specs=[pl.BlockSpec((N_HEADS, None, 2 * HEAD_DIM, tm),
                                lambda i: (0, i // per_tile, 0, i % per_tile)),
                   pl.BlockSpec((N_HEADS, tm, 2 * HEAD_DIM), lambda i: (0, i, 0))],
        out_shape=[jax.ShapeDtypeStruct((N_HEADS, s // tq, 2 * HEAD_DIM, tq), BF16),
                   jax.ShapeDtypeStruct((N_HEADS, s, 2 * HEAD_DIM), BF16)],
        scratch_shapes=[pltpu.VMEM((1, LANES), F32)],
        compiler_params=_cparams(("arbitrary",), 32),
        name="forget_prep",
    )(x_bf, w_in_t, b_fg, proj, proj)


def _transpose_kernel(fox_v_ref, diff_q_ref, diff_v_ref, o_ref, *, tk):
    for g, ref in enumerate((fox_v_ref, diff_q_ref, diff_v_ref)):
        xt = ref[...].astype(F32).T
        for c in range(o_ref.shape[1]):
            o_ref[g, c] = xt[:, c * tk:(c + 1) * tk].astype(BF16)


def _transpose_segs(plain, rope, tm, tk):
    s = plain.shape[0]
    seg_spec = lambda blk: pl.BlockSpec((tm, SEG), lambda i: (i, blk))
    return pl.pallas_call(
        functools.partial(_transpose_kernel, tk=tk),
        grid=(s // tm,),
        in_specs=[seg_spec(PLAIN_SEGS.index(2)), seg_spec(ROPE_SEGS.index(3)), seg_spec(PLAIN_SEGS.index(5))],
        out_specs=pl.BlockSpec((3, tm // tk, SEG, tk), lambda i: (0, i, 0, 0)),
        out_shape=jax.ShapeDtypeStruct((3, s // tk, SEG, tk), BF16),
        compiler_params=_cparams(("arbitrary",), 40),
        name="transpose_segs",
    )(plain, rope, plain)


ONES_ROWS = 16


class _AttnState:
    S_SLOTS = 4
    P_SLOTS = 2
    N_REFS = 2 * S_SLOTS + 2 * P_SLOTS + 2

    def __init__(self, refs):
        a, b = self.S_SLOTS, self.S_SLOTS + self.P_SLOTS
        self.s = refs[0:a]
        self.p = refs[a:b]
        self.alpha = refs[b:b + self.P_SLOTS]
        c = b + self.P_SLOTS
        self.chunk_max = refs[c:c + self.S_SLOTS]
        self.m, self.acc = refs[c + self.S_SLOTS:]

    @classmethod
    def scratch(cls, tq, tk):
        row = pltpu.VMEM((1, tq), F32)
        return ([pltpu.VMEM((tk, tq), F32)] * cls.S_SLOTS + [pltpu.VMEM((tk, tq), BF16)] * cls.P_SLOTS
                + [row] * (cls.P_SLOTS + cls.S_SLOTS) + [row, pltpu.VMEM((HEAD_DIM + ONES_ROWS, tq), F32)])

    def init(self):
        self.m[...] = jnp.full_like(self.m, -jnp.inf)
        self.acc[...] = jnp.zeros_like(self.acc)
        self.p[1][...] = jnp.zeros_like(self.p[1])
        self.alpha[1][...] = jnp.ones_like(self.alpha[1])

    def normalized(self):
        return self.acc[0:HEAD_DIM, :] / self.acc[HEAD_DIM:HEAD_DIM + 1, :]


def _attn_pipeline(i, q_t, k_ref, v_t_ref, states, tq, tk):
    n_sub = tq // tk
    assert tq == n_sub * tk and n_sub == _AttnState.S_SLOTS

    def scores(c, slot, q0=0):
        k = k_ref[pl.ds(pl.multiple_of(c * tk, tk), tk), :]
        for st, q in zip(states, q_t):
            s = jnp.dot(k, q[:, q0:], preferred_element_type=F32)
            st.s[slot][:, q0:] = s
            st.chunk_max[slot][:, q0:] = jnp.max(s, axis=0, keepdims=True)

    def softmax(s_slot, slot, k0=None, q0=0):
        for st in states:
            s = st.s[s_slot][:, q0:]
            if k0 is None:
                chunk_max = st.chunk_max[s_slot][:, q0:]
            else:
                key = lax.broadcasted_iota(jnp.int32, s.shape, 0) + k0
                query = lax.broadcasted_iota(jnp.int32, s.shape, 1) + (i * tq + q0)
                s = jnp.where(key <= query, s, -jnp.inf)
                chunk_max = jnp.max(s, axis=0, keepdims=True)
            m_prev = st.m[:, q0:]
            m_new = jnp.maximum(m_prev, chunk_max)
            st.m[:, q0:] = m_new
            st.alpha[slot][:, q0:] = jnp.exp2(m_prev - m_new)
            st.p[slot][:, q0:] = jnp.exp2(s - m_new).astype(BF16)

    def values(c, slot, q0=0):
        v_t = jnp.concatenate([v_t_ref[c], jnp.ones((ONES_ROWS, tk), BF16)], axis=0)
        for st in states:
            st.acc[:, q0:] = st.alpha[slot][:, q0:] * st.acc[:, q0:] + jnp.dot(
                v_t, st.p[slot][:, q0:], preferred_element_type=F32)

    for st in states:
        st.init()
    scores(0, 0)
    scores(1, 1)

    def body(block, carry):
        c = n_sub * block
        for u in range(n_sub):
            scores(c + u + 2, (u + 2) % n_sub)
            softmax(u, u % 2)
            values(jnp.maximum(c + u - 1, 0), (u + 1) % 2)
        return carry

    lax.fori_loop(0, i, body, 0)
    c = n_sub * i
    for u in range(n_sub):
        if u + 2 < n_sub:
            scores(c + u + 2, u + 2, q0=(u + 2) * tk)
        softmax(u, u % 2, k0=(c + u) * tk, q0=u * tk)
        values(jnp.maximum(c + u - 1, 0), (u + 1) % 2, q0=max(u - 1, 0) * tk)
    values(c + n_sub - 1, (n_sub - 1) % 2, q0=(n_sub - 1) * tk)


def _fox_attn_kernel(q_ref, k_ref, v_ref, o_ref, *scratch, tq, tk):
    st = _AttnState(scratch)
    _attn_pipeline(pl.program_id(1), [q_ref[...]], k_ref, v_ref, [st], tq, tk)
    o_ref[...] = st.normalized().T.astype(BF16)


def _fox_attn(q_aug_t, k_aug, segs_t, tq, tk):
    _, s, _ = k_aug.shape
    return pl.pallas_call(
        functools.partial(_fox_attn_kernel, tq=tq, tk=tk),
        grid=(N_HEADS, s // tq),
        in_specs=[
            pl.BlockSpec((None, None, 2 * HEAD_DIM, tq), lambda h, i: (h, i, 0, 0)),
            pl.BlockSpec((None, s, 2 * HEAD_DIM), lambda h, i: (h, 0, 0)),
            pl.BlockSpec((None, s // tk, HEAD_DIM, tk), lambda h, i: (0, 0, h, 0)),
        ],
        out_specs=pl.BlockSpec((tq, HEAD_DIM), lambda h, i: (i, h)),
        out_shape=jax.ShapeDtypeStruct((s, SEG), BF16),
        scratch_shapes=_AttnState.scratch(tq, tk),
        compiler_params=_cparams(("arbitrary", "arbitrary"), 40),
        name="fox_attn",
    )(q_aug_t, k_aug, segs_t)


def _diff_attn_kernel(q_ref, k_ref, v_ref, dl_ref, gain_ref, o_ref, *scratch, tq, tk, lambda_init):
    st1 = _AttnState(scratch[:_AttnState.N_REFS])
    st2 = _AttnState(scratch[_AttnState.N_REFS:])
    q_t = jnp.concatenate([q_ref[c] for c in range(tq // tk)], axis=1)
    channel = lax.broadcasted_iota(jnp.int32, (HEAD_DIM, 1), 0)
    q1 = jnp.where(channel < QK_DIM, q_t, jnp.zeros_like(q_t))
    q2 = jnp.where(channel >= QK_DIM, q_t, jnp.zeros_like(q_t))
    _attn_pipeline(pl.program_id(1), [q1, q2], k_ref, v_ref, [st1, st2], tq, tk)

    dl = dl_ref[...]
    lam = (jnp.exp(jnp.sum(dl[0:1, :] * dl[1:2, :], axis=1, keepdims=True))
           - jnp.exp(jnp.sum(dl[2:3, :] * dl[3:4, :], axis=1, keepdims=True)) + lambda_init)
    o = st1.normalized() - lam * st2.normalized()
    o = o * lax.rsqrt(jnp.mean(o * o, axis=0, keepdims=True) + LN_EPS)
    o_ref[...] = (o.T * gain_ref[...] * (1.0 - lambda_init)).astype(BF16)


def _diff_attn(rope, segs_t, diff_lambda, diff_gain, layer, lambda_init, tq, tk):
    s = rope.shape[0]
    k_block0 = ROPE_SEGS.index(4) * SEG // HEAD_DIM
    return pl.pallas_call(
        functools.partial(_diff_attn_kernel, tq=tq, tk=tk, lambda_init=lambda_init),
        grid=(N_HEADS, s // tq),
        in_specs=[
            pl.BlockSpec((None, tq // tk, HEAD_DIM, tk), lambda h, i: (1, i, h, 0)),
            pl.BlockSpec((s, HEAD_DIM), lambda h, i: (0, k_block0 + h)),
            pl.BlockSpec((None, s // tk, HEAD_DIM, tk), lambda h, i: (2, 0, h, 0)),
            pl.BlockSpec((None, 4, QK_DIM), lambda h, i: (layer, 0, 0)),
            pl.BlockSpec((None, 1, HEAD_DIM), lambda h, i: (layer, 0, 0)),
        ],
        out_specs=pl.BlockSpec((tq, HEAD_DIM), lambda h, i: (i, h)),
        out_shape=jax.ShapeDtypeStruct((s, SEG), BF16),
        scratch_shapes=_AttnState.scratch(tq, tk) * 2,
        compiler_params=_cparams(("arbitrary", "arbitrary"), 40),
        name="diff_attn",
    )(segs_t, rope, segs_t, diff_lambda, diff_gain)


def _pool_kernel(u_ref, halo_ref, wg_ref, sc_ref, o_ref, *, tm):
    i = pl.program_id(0)
    tokens_seen = (lax.broadcasted_iota(jnp.int32, (tm, 1), 0) + i * tm + 1).astype(F32)
    for g, window in enumerate(POOL_WINDOWS):
        cols = slice(g * POOL_GROUP_DIM, (g + 1) * POOL_GROUP_DIM)
        u = u_ref[:, cols].astype(F32)
        halo = halo_ref[:, cols].astype(F32)
        halo = jnp.where(i > 0, halo, jnp.zeros_like(halo))
        ext = jnp.concatenate([halo, u], axis=0)
        shift = 1
        while shift < window:
            ext = ext + pltpu.roll(ext, shift, 0)
            shift *= 2
        pooled = ext[POOL_HALO:, :] / jnp.minimum(tokens_seen, float(window))
        delta = (pooled - u).astype(BF16)
        y = jnp.dot(delta, wg_ref[g].astype(BF16), preferred_element_type=F32) * sc_ref[:, cols]
        o_ref[:, cols] = y.astype(BF16)


def _pool(plain, w_pool_group, pool_scale, layer, tm):
    s = plain.shape[0]
    n_groups = len(POOL_WINDOWS)
    halo_blocks_per_tile = tm // POOL_HALO
    blk = PLAIN_SEGS.index(6)
    return pl.pallas_call(
        functools.partial(_pool_kernel, tm=tm),
        grid=(s // tm,),
        in_specs=[
            pl.BlockSpec((tm, SEG), lambda i: (i, blk)),
            pl.BlockSpec((POOL_HALO, SEG), lambda i: (jnp.maximum(i * halo_blocks_per_tile - 1, 0), blk)),
            pl.BlockSpec((None, n_groups, POOL_GROUP_DIM, POOL_GROUP_DIM), lambda i: (layer, 0, 0, 0)),
            pl.BlockSpec((None, 1, SEG), lambda i: (layer, 0, 0)),
        ],
        out_specs=pl.BlockSpec((tm, SEG), lambda i: (i, 0)),
        out_shape=jax.ShapeDtypeStruct((s, SEG), BF16),
        compiler_params=_cparams(("arbitrary",), 32),
        name="pool",
    )(plain, plain, w_pool_group, pool_scale)


def _merge_kernel(ya_ref, yb_ref, yc_ref, wa_ref, wb_ref, wc_ref, ga_ref, gb_ref, gc_ref, o_ref,
                  wab_ref, wbb_ref, wcb_ref):
    i = pl.program_id(1)

    def emit(ws):
        h = ga_ref[...].astype(F32) * jnp.dot(ya_ref[...], ws[0], preferred_element_type=F32)
        h = h + gb_ref[...].astype(F32) * jnp.dot(yb_ref[...], ws[1], preferred_element_type=F32)
        h = h + gc_ref[...].astype(F32) * jnp.dot(yc_ref[...], ws[2], preferred_element_type=F32)
        o_ref[...] = h.astype(BF16)

    _with_bf16_weights(i == 0, i != 0, [wa_ref, wb_ref, wc_ref], [wab_ref, wbb_ref, wcb_ref], emit)


def _merge(y_a, y_b, y_c, w_a, w_b, w_c, gates, layer, tm, tn):
    s = y_a.shape[0]
    gate_blocks = D_MODEL // tn
    y_spec = pl.BlockSpec((tm, SEG), lambda j, i: (i, 0))
    w_spec = pl.BlockSpec((None, SEG, tn), lambda j, i: (layer, 0, j))
    gate_spec = lambda br: pl.BlockSpec((tm, tn), lambda j, i: (i, br * gate_blocks + j))
    return pl.pallas_call(
        _merge_kernel,
        grid=(D_MODEL // tn, s // tm),
        in_specs=[y_spec, y_spec, y_spec, w_spec, w_spec, w_spec, gate_spec(0), gate_spec(1), gate_spec(2)],
        out_specs=pl.BlockSpec((tm, tn), lambda j, i: (i, j)),
        out_shape=jax.ShapeDtypeStruct((s, D_MODEL), BF16),
        scratch_shapes=[pltpu.VMEM((SEG, tn), BF16)] * 3,
        compiler_params=_cparams(("arbitrary", "arbitrary"), 48),
        name="merge",
    )(y_a, y_b, y_c, w_a, w_b, w_c, gates, gates, gates)


def _layer_norm_rows(z, g, b):
    mu = jnp.mean(z, axis=1, keepdims=True)
    zc = z - mu
    var = jnp.mean(zc * zc, axis=1, keepdims=True)
    return zc * lax.rsqrt(var + LN_EPS) * g + b


def _matmul_ln_kernel(h_ref, w_ref, x_ref, g_ref, b_ref, xo_ref, xb_ref, wbf_ref, *, alpha):
    i = pl.program_id(0)

    def emit(ws):
        mix = jnp.dot(h_ref[...], ws[0], preferred_element_type=F32)
        y = _layer_norm_rows(alpha * x_ref[...] + mix, g_ref[...], b_ref[...])
        xo_ref[...] = y
        xb_ref[...] = y.astype(BF16)

    _with_bf16_weights(i == 0, i != 0, [w_ref], [wbf_ref], emit)


def _matmul_ln(h, w, x, g, b, layer, alpha, tm):
    s, k = h.shape
    n = w.shape[-1]
    row_spec = pl.BlockSpec((tm, n), lambda i: (i, 0))
    par_spec = pl.BlockSpec((None, 1, n), lambda i: (layer, 0, 0))
    return pl.pallas_call(
        functools.partial(_matmul_ln_kernel, alpha=alpha),
        grid=(s // tm,),
        in_specs=[pl.BlockSpec((tm, k), lambda i: (i, 0)),
                  pl.BlockSpec((None, k, n), lambda i: (layer, 0, 0), pipeline_mode=pl.Buffered(1)),
                  row_spec, par_spec, par_spec],
        out_specs=[row_spec, row_spec],
        out_shape=[jax.ShapeDtypeStruct((s, n), F32), jax.ShapeDtypeStruct((s, n), BF16)],
        scratch_shapes=[pltpu.VMEM((k, n), BF16)],
        compiler_params=_cparams(("arbitrary",), 56),
        name="outproj_ln",
    )(h, w, x, g, b)


def _add_ln_kernel(x_ref, f_ref, g_ref, b_ref, xo_ref, xb_ref, *, alpha):
    y = _layer_norm_rows(alpha * x_ref[...] + f_ref[...], g_ref[...], b_ref[...])
    xo_ref[...] = y
    xb_ref[...] = y.astype(BF16)


def _add_ln(x, f, g, b, layer, alpha, tm):
    s, d = x.shape
    row_spec = pl.BlockSpec((tm, d), lambda i: (i, 0))
    par_spec = pl.BlockSpec((None, 1, d), lambda i: (layer, 0, 0))
    return pl.pallas_call(
        functools.partial(_add_ln_kernel, alpha=alpha),
        grid=(s // tm,),
        in_specs=[row_spec, row_spec, par_spec, par_spec],
        out_specs=[row_spec, row_spec],
        out_shape=[jax.ShapeDtypeStruct((s, d), F32), jax.ShapeDtypeStruct((s, d), BF16)],
        compiler_params=_cparams(("arbitrary",), 32),
        name="add_ln",
    )(x, f, g, b)


def _gate_up_kernel(te_ref, tf_ref, nu_ref, x_ref, wg_ref, wu_ref, o_ref, wgb_ref, wub_ref):
    t = pl.program_id(1)

    active = t < nu_ref[0]
    first = tf_ref[t] == 1

    def emit(ws):
        x = x_ref[...]
        g = jnp.dot(x, ws[0], preferred_element_type=F32)
        u = jnp.dot(x, ws[1], preferred_element_type=F32)
        o_ref[...] = (g * jax.nn.sigmoid(g) * u).astype(BF16)

    _with_bf16_weights(active & first, active & jnp.logical_not(first), [wg_ref, wu_ref], [wgb_ref, wub_ref], emit)

    @pl.when(t >= nu_ref[0])
    def _unused_tile():
        o_ref[...] = jnp.zeros_like(o_ref)


def _gate_up(groups, x_sorted, w_gate, w_up, layer, tm, tn):
    tile_expert, tile_first, n_used = groups
    n_slots, k = x_sorted.shape
    n = w_gate.shape[-1]
    row = lambda j, t, te, tf, nu: (jnp.minimum(t, nu[0] - 1), 0)
    w_spec = pl.BlockSpec((None, None, k, tn), lambda j, t, te, tf, nu: (layer, te[t], 0, j))
    return pl.pallas_call(
        _gate_up_kernel,
        grid_spec=pltpu.PrefetchScalarGridSpec(
            num_scalar_prefetch=3,
            grid=(n // tn, n_slots // tm),
            in_specs=[pl.BlockSpec((tm, k), row), w_spec, w_spec],
            out_specs=pl.BlockSpec((tm, tn), lambda j, t, te, tf, nu: (t, j)),
            scratch_shapes=[pltpu.VMEM((k, tn), BF16)] * 2,
        ),
        out_shape=jax.ShapeDtypeStruct((n_slots, n), BF16),
        compiler_params=_cparams(("arbitrary", "arbitrary"), 48),
        name="ffn_gate_up",
    )(tile_expert, tile_first, n_used, x_sorted, w_gate, w_up)


def _down_kernel(te_ref, tf_ref, nu_ref, h_ref, w_ref, o_ref, wbf_ref):
    t = pl.program_id(1)

    active = t < nu_ref[0]
    first = tf_ref[t] == 1

    def emit(ws):
        o_ref[...] = jnp.dot(h_ref[...], ws[0], preferred_element_type=F32)

    _with_bf16_weights(active & first, active & jnp.logical_not(first), [w_ref], [wbf_ref], emit)

    @pl.when(t >= nu_ref[0])
    def _unused_tile():
        o_ref[...] = jnp.zeros_like(o_ref)


def _down(groups, h_sorted, w_down, layer, tm, tn):
    tile_expert, tile_first, n_used = groups
    n_slots, k = h_sorted.shape
    n = w_down.shape[-1]
    return pl.pallas_call(
        _down_kernel,
        grid_spec=pltpu.PrefetchScalarGridSpec(
            num_scalar_prefetch=3,
            grid=(n // tn, n_slots // tm),
            in_specs=[pl.BlockSpec((tm, k), lambda j, t, te, tf, nu: (jnp.minimum(t, nu[0] - 1), 0)),
                      pl.BlockSpec((None, None, k, tn), lambda j, t, te, tf, nu: (layer, te[t], 0, j))],
            out_specs=pl.BlockSpec((tm, tn), lambda j, t, te, tf, nu: (t, j)),
            scratch_shapes=[pltpu.VMEM((k, tn), BF16)],
        ),
        out_shape=jax.ShapeDtypeStruct((n_slots, n), F32),
        compiler_params=_cparams(("arbitrary", "arbitrary"), 52),
        name="ffn_down",
    )(tile_expert, tile_first, n_used, h_sorted, w_down)


def _single_group(n_tiles):
    return (jnp.zeros((n_tiles,), jnp.int32),
            jnp.zeros((n_tiles,), jnp.int32).at[0].set(1),
            jnp.full((1,), n_tiles, jnp.int32))


def _router_kernel(x_ref, rw_ref, rb_ref, o_ref, cnt_ref, carry_ref, *, tm):
    i = pl.program_id(0)

    @pl.when(i == 0)
    def _init():
        carry_ref[...] = jnp.zeros_like(carry_ref)

    logits = jnp.dot(x_ref[...], rw_ref[...], preferred_element_type=F32,
                     precision=lax.Precision.HIGHEST) + rb_ref[...]
    lane = lax.broadcasted_iota(jnp.int32, (tm, LANES), 1).astype(F32)
    lg = jnp.where(lane < N_EXPERTS, logits, -jnp.inf)
    v1 = jnp.max(lg, axis=1, keepdims=True)
    e1 = jnp.min(jnp.where(lg == v1, lane, float(LANES)), axis=1, keepdims=True)
    lg2 = jnp.where(lane == e1, -jnp.inf, lg)
    v2 = jnp.max(lg2, axis=1, keepdims=True)
    e2 = jnp.min(jnp.where(lg2 == v2, lane, float(LANES)), axis=1, keepdims=True)
    t2 = jnp.exp(v2 - v1)
    w1 = 1.0 / (1.0 + t2)
    w2 = t2 / (1.0 + t2)
    pick1 = lane == e1
    pick2 = lane == e2
    onehot = jnp.where(pick1 | pick2, 1.0, 0.0)
    row = lax.broadcasted_iota(jnp.int32, (tm, tm), 0)
    col = lax.broadcasted_iota(jnp.int32, (tm, tm), 1)
    before = (col < row).astype(BF16)
    seen = jnp.dot(before, onehot.astype(BF16), preferred_element_type=F32) + carry_ref[...]
    r1 = jnp.sum(jnp.where(pick1, seen, 0.0), axis=1, keepdims=True)
    r2 = jnp.sum(jnp.where(pick2, seen, 0.0), axis=1, keepdims=True)
    carry_ref[...] = carry_ref[...] + jnp.sum(onehot, axis=0, keepdims=True)
    o_ref[...] = jnp.where(lane == 0, e1, jnp.where(lane == 1, e2, jnp.where(lane == 2, w1,
                 jnp.where(lane == 3, w2, jnp.where(lane == 4, r1, jnp.where(lane == 5, r2, 0.0))))))
    cnt_ref[...] = carry_ref[...]


def _router(x, rw_pad, rb_pad, tm):
    s, d = x.shape
    return pl.pallas_call(
        functools.partial(_router_kernel, tm=tm),
        grid=(s // tm,),
        in_specs=[pl.BlockSpec((tm, d), lambda i: (i, 0)),
                  pl.BlockSpec((d, LANES), lambda i: (0, 0)),
                  pl.BlockSpec((1, LANES), lambda i: (0, 0))],
        out_specs=[pl.BlockSpec((tm, LANES), lambda i: (i, 0)), pl.BlockSpec((1, LANES), lambda i: (0, 0))],
        out_shape=[jax.ShapeDtypeStruct((s, LANES), F32), jax.ShapeDtypeStruct((1, LANES), F32)],
        scratch_shapes=[pltpu.VMEM((1, LANES), F32)],
        compiler_params=_cparams(("arbitrary",), 32),
        name="router",
    )(x, rw_pad, rb_pad)


def _scatter_copies(x_ref, o_hbm, p1_ref, p2_ref, base, r, sem):
    return (pltpu.make_async_copy(x_ref.at[r], o_hbm.at[p1_ref[base + r]], sem),
            pltpu.make_async_copy(x_ref.at[r], o_hbm.at[p2_ref[base + r]], sem))


def _dispatch_kernel(p1_ref, p2_ref, x_ref, init_hbm, o_hbm, sem, *, tt):
    del init_hbm
    base = pl.program_id(0) * tt

    def issue(r, carry):
        for cp in _scatter_copies(x_ref, o_hbm, p1_ref, p2_ref, base, r, sem):
            cp.start()
        return carry

    lax.fori_loop(0, tt, issue, 0, unroll=8)

    def drain(r, carry):
        for cp in _scatter_copies(x_ref, o_hbm, p1_ref, p2_ref, base, r, sem):
            cp.wait()
        return carry

    lax.fori_loop(0, tt, drain, 0, unroll=8)


def _dispatch(pos1, pos2, x_bf, n_slots, tt):
    s, d = x_bf.shape
    x_slabs = x_bf.reshape(s, d // LANES, LANES)
    out = pl.pallas_call(
        functools.partial(_dispatch_kernel, tt=tt),
        grid_spec=pltpu.PrefetchScalarGridSpec(
            num_scalar_prefetch=2,
            grid=(s // tt,),
            in_specs=[pl.BlockSpec((tt, d // LANES, LANES), lambda i, p1, p2: (i, 0, 0)),
                      pl.BlockSpec(memory_space=pl.ANY)],
            out_specs=pl.BlockSpec(memory_space=pl.ANY),
            scratch_shapes=[pltpu.SemaphoreType.DMA],
        ),
        out_shape=jax.ShapeDtypeStruct((n_slots, d // LANES, LANES), BF16),
        input_output_aliases={3: 0},
        compiler_params=_cparams(("arbitrary",), 32),
        name="moe_dispatch",
    )(pos1, pos2, x_slabs, jnp.zeros((n_slots, d // LANES, LANES), BF16))
    return out.reshape(n_slots, d)


def _row_copy(src_hbm, src_row, buf, dst_row, sem):
    return pltpu.make_async_copy(src_hbm.at[pl.ds(src_row, 1), :], buf.at[pl.ds(dst_row, 1), :], sem)


def _combine_ln_kernel(p1_ref, p2_ref, y_hbm, x_ref, route_ref, g_ref, b_ref, xo_ref, xb_ref,
                       buf, sem, *, tc, alpha):
    i = pl.program_id(0)

    def copies(step, slot, r):
        t = step * tc + r
        return (_row_copy(y_hbm, p1_ref[t], buf.at[slot, 0], r, sem.at[slot]),
                _row_copy(y_hbm, p2_ref[t], buf.at[slot, 1], r, sem.at[slot]))

    def issue(step, slot):
        def body(r, carry):
            for cp in copies(step, slot, r):
                cp.start()
            return carry

        lax.fori_loop(0, tc, body, 0, unroll=8)

    @pl.when(i == 0)
    def _first():
        issue(0, 0)

    @pl.when(i + 1 < pl.num_programs(0))
    def _prefetch():
        issue(i + 1, (i + 1) % 2)

    slot = i % 2

    def drain(r, carry):
        for cp in copies(i, slot, r):
            cp.wait()
        return carry

    lax.fori_loop(0, tc, drain, 0, unroll=8)
    route = route_ref[...]
    f = route[:, 2:3] * buf[slot, 0] + route[:, 3:4] * buf[slot, 1]
    y = _layer_norm_rows(alpha * x_ref[...] + f, g_ref[...], b_ref[...])
    xo_ref[...] = y
    xb_ref[...] = y.astype(BF16)


def _combine_ln(pos1, pos2, y_sorted, x, route, g, b, layer, alpha, tc):
    s, d = x.shape
    row_spec = pl.BlockSpec((tc, d), lambda i, p1, p2: (i, 0))
    par_spec = pl.BlockSpec((None, 1, d), lambda i, p1, p2: (layer, 0, 0))
    return pl.pallas_call(
        functools.partial(_combine_ln_kernel, tc=tc, alpha=alpha),
        grid_spec=pltpu.PrefetchScalarGridSpec(
            num_scalar_prefetch=2,
            grid=(s // tc,),
            in_specs=[pl.BlockSpec(memory_space=pl.ANY), row_spec,
                      pl.BlockSpec((tc, LANES), lambda i, p1, p2: (i, 0)), par_spec, par_spec],
            out_specs=[row_spec, row_spec],
            scratch_shapes=[pltpu.VMEM((2, 2, tc, d), F32), pltpu.SemaphoreType.DMA((2,))],
        ),
        out_shape=[jax.ShapeDtypeStruct((s, d), F32), jax.ShapeDtypeStruct((s, d), BF16)],
        compiler_params=_cparams(("arbitrary",), 32),
        name="moe_combine_ln",
    )(pos1, pos2, y_sorted, x, route, g, b)


def _moe_plan(route, counts, tm, n_tiles):
    expert = route[:, 0:2].astype(jnp.int32)
    rank = route[:, 4:6].astype(jnp.int32)
    count = counts[0, :N_EXPERTS].astype(jnp.int32)
    padded = ((count + tm - 1) // tm) * tm
    ends = jnp.cumsum(padded)
    starts = ends - padded
    pos = starts[expert] + rank
    n_used = ends[-1] // tm
    tile = jnp.arange(n_tiles, dtype=jnp.int32)
    last = jnp.minimum(tile, n_used - 1)
    tile_expert = jnp.sum(last[:, None] * tm >= ends[None, :], axis=1).astype(jnp.int32)
    prev = jnp.concatenate([jnp.full((1,), -1, jnp.int32), tile_expert[:-1]])
    tile_first = (tile_expert != prev).astype(jnp.int32)
    return pos, (tile_expert, tile_first, n_used.reshape(1).astype(jnp.int32))


def _rope_tables(seq):
    pos = jnp.arange(seq, dtype=F32)
    inv = ROPE_THETA ** (-jnp.arange(0, QK_DIM, 2, dtype=F32) / QK_DIM)
    ang = pos[:, None] * inv[None, :]
    cos, sin = jnp.cos(ang), jnp.sin(ang)
    return jnp.tile(cos, (1, 4)), jnp.tile(jnp.concatenate([-sin, sin], axis=1), (1, 2))


def _pad_lanes(a):
    return jnp.pad(a, ((0, 0), (0, LANES - a.shape[1])))


def kernel(x, w_in, b_forget, w_pool_group, pool_scale, diff_lambda, diff_norm_gain, w_branch_a, w_branch_b, w_branch_c, b_gate, w_out, ln1_g, ln1_b, ln2_g, ln2_b, ffn_w_gate, ffn_w_up, ffn_w_down, router_w, router_b, expert_w_gate, expert_w_up, expert_w_down):
    batch, s, d = x.shape
    assert batch == 1 and d == D_MODEL and w_in.shape[-1] == N_MAIN + N_HEADS
    depth = w_in.shape[0]
    alpha = (2 * depth) ** 0.25

    tm = min(1024, s)
    tq = min(1024, s)
    tk = tq // 4
    te = min(512, s)
    tg = min(256, s)
    n_ff = 512
    n_tiles = 2 * s // te + N_EXPERTS
    n_slots = n_tiles * te

    cos_t, sin_t = _rope_tables(s)
    w_in_t = jnp.swapaxes(w_in, 1, 2)
    b_gate_flat = b_gate.reshape(depth, 1, 3 * D_MODEL)
    pool_scale3 = pool_scale.reshape(depth, 1, SEG)
    gain3 = diff_norm_gain.reshape(depth, 1, HEAD_DIM)
    ln = [p.reshape(depth, 1, D_MODEL) for p in (ln1_g, ln1_b, ln2_g, ln2_b)]
    dense_w = [w.reshape(w.shape[0], 1, *w.shape[1:]) for w in (ffn_w_gate, ffn_w_up, ffn_w_down)]

    xf = x.reshape(s, d)
    xb = xf.astype(BF16)
    for layer in range(depth):
        lambda_init = 0.8 - 0.6 * math.exp(-0.3 * layer)
        plain, rope, gates = _inproj_all(xb, w_in_t, b_gate_flat, cos_t, sin_t, layer, tm)
        b_fg = _pad_lanes(b_forget[layer].reshape(1, N_HEADS))
        q_aug_t, k_aug = _forget_prep(xb, w_in_t, b_fg, plain, layer, te, tq)
        segs_t = _transpose_segs(plain, rope, te, tk)
        y_a = _fox_attn(q_aug_t, k_aug, segs_t, tq, tk)
        y_b = _pool(plain, w_pool_group, pool_scale3, layer, tm)
        y_c = _diff_attn(rope, segs_t, diff_lambda, gain3, layer, lambda_init, tq, tk)
        h = _merge(y_a, y_b, y_c, w_branch_a, w_branch_b, w_branch_c, gates, layer, tm, 512)
        xf, xb = _matmul_ln(h, w_out, xf, ln[0], ln[1], layer, alpha, tg)

        j = layer // 2
        if layer % 2 == 0:
            hidden = _gate_up(_single_group(s // tm), xb, dense_w[0], dense_w[1], j, tm, n_ff)
            f = _down(_single_group(s // te), hidden, dense_w[2], j, te, 512)
            xf, xb = _add_ln(xf, f, ln[2], ln[3], layer, alpha, tg)
        else:
            route, counts = _router(xf, _pad_lanes(router_w[j]), _pad_lanes(router_b[j].reshape(1, N_EXPERTS)), te)
            pos, groups = _moe_plan(route, counts, te, n_tiles)
            x_sorted = _dispatch(pos[:, 0], pos[:, 1], xb, n_slots, te)
            hidden = _gate_up(groups, x_sorted, expert_w_gate, expert_w_up, j, te, n_ff)
            y_sorted = _down(groups, hidden, expert_w_down, j, te, 512)
            xf, xb = _combine_ln(pos[:, 0], pos[:, 1], y_sorted, xf, route, ln[2], ln[3], layer, alpha, tg)
    return xf.reshape(batch, s, d)
```

```python
import functools
import math

import jax
import jax.numpy as jnp
from jax import lax
from jax.experimental import pallas as pl
from jax.experimental.pallas import tpu as pltpu

BF16 = jnp.bfloat16
F32 = jnp.float32

D_MODEL = 2048
N_HEADS = 8
HEAD_DIM = 128
QK_DIM = 64
SEG = 1024
N_MAIN = 13 * SEG
POOL_WINDOWS = (2, 4, 8, 16)
POOL_GROUP_DIM = 256
POOL_HALO = 16
N_EXPERTS = 8
ROPE_THETA = 10000.0
LN_EPS = 1e-5
LOG2E = math.log2(math.e)
FOX_Q_SCALE = HEAD_DIM ** -0.5 * LOG2E
DIFF_Q_SCALE = QK_DIM ** -0.5 * LOG2E
LANES = 128

MIB = 2 ** 20


def _cparams(semantics, vmem_mib):
    return pltpu.CompilerParams(dimension_semantics=semantics, vmem_limit_bytes=vmem_mib * MIB)


def _with_bf16_weights(fresh, cached, w_refs, wbf_refs, emit, transpose=False):
    @pl.when(fresh)
    def _fresh():
        ws = [(w[...].T if transpose else w[...]).astype(BF16) for w in w_refs]
        for dst, w in zip(wbf_refs, ws):
            dst[...] = w
        emit(ws)

    @pl.when(cached)
    def _cached():
        emit([r[...] for r in wbf_refs])


def _rope_store(acc, cos, sin, scale, o_ref):
    lane = lax.broadcasted_iota(jnp.int32, (1, LANES), 1)
    first_half = (lane % QK_DIM) < (QK_DIM // 2)
    for c in range(SEG // LANES):
        a = acc[:, c * LANES:(c + 1) * LANES]
        partner = jnp.where(first_half, pltpu.roll(a, LANES - QK_DIM // 2, 1), pltpu.roll(a, QK_DIM // 2, 1))
        o_ref[:, c * LANES:(c + 1) * LANES] = ((a * cos + partner * sin) * scale).astype(BF16)


def _inproj_kernel(x_ref, w_ref, *rest, mode, first_scale):
    o_ref, wbf_ref = rest[-2:]
    j = pl.program_id(0)
    i = pl.program_id(1)

    def emit(ws):
        acc = jnp.dot(x_ref[...], ws[0], preferred_element_type=F32)
        scale = jnp.where(j == 0, first_scale, 1.0).astype(F32)
        if mode == "plain":
            o_ref[...] = (acc * scale).astype(BF16)
        elif mode == "rope":
            _rope_store(acc, rest[0][...], rest[1][...], scale, o_ref)
        else:
            o_ref[...] = jax.nn.sigmoid(acc + rest[0][...]).astype(BF16)

    _with_bf16_weights(i == 0, i != 0, [w_ref], [wbf_ref], emit, transpose=True)


PLAIN_SEGS = (0, 1, 2, 5, 6)
ROPE_SEGS = (3, 4)
GATE_SEGS = (7, 8, 9, 10, 11, 12)


def _inproj(x_bf, w_in_t, extras, extra_specs, layer, tm, mode, segs, first_scale):
    s, d = x_bf.shape
    jump = next((k for k in range(1, len(segs)) if segs[k] != segs[k - 1] + 1), len(segs))
    gap = segs[jump] - segs[jump - 1] - 1 if jump < len(segs) else 0
    seg_of = lambda j: segs[0] + j + jnp.where(j >= jump, gap, 0)
    return pl.pallas_call(
        functools.partial(_inproj_kernel, mode=mode, first_scale=first_scale),
        grid=(len(segs), s // tm),
        in_specs=[pl.BlockSpec((tm, d), lambda j, i: (i, 0)),
                  pl.BlockSpec((None, SEG, d), lambda j, i: (layer, seg_of(j), 0))] + extra_specs,
        out_specs=pl.BlockSpec((tm, SEG), lambda j, i: (i, j)),
        out_shape=jax.ShapeDtypeStruct((s, len(segs) * SEG), BF16),
        scratch_shapes=[pltpu.VMEM((d, SEG), BF16)],
        compiler_params=_cparams(("arbitrary", "arbitrary"), 48),
        name="inproj_" + mode,
    )(x_bf, w_in_t, *extras)


def _inproj_all(x_bf, w_in_t, b_gate_flat, cos_t, sin_t, layer, tm):
    table_spec = pl.BlockSpec((tm, LANES), lambda j, i: (i, 0))
    plain = _inproj(x_bf, w_in_t, [], [], layer, tm, "plain", PLAIN_SEGS, FOX_Q_SCALE)
    rope = _inproj(x_bf, w_in_t, [cos_t, sin_t], [table_spec, table_spec], layer, tm, "rope", ROPE_SEGS,
                   DIFF_Q_SCALE)
    gates = _inproj(x_bf, w_in_t, [b_gate_flat], [pl.BlockSpec((None, 1, SEG), lambda j, i: (layer, 0, j))],
                    layer, tm, "gate", GATE_SEGS, 1.0)
    return plain, rope, gates


def _split3(v):
    hi = v.astype(BF16)
    r = v - hi.astype(F32)
    mid = r.astype(BF16)
    lo = (r - mid.astype(F32)).astype(BF16)
    return hi, mid, lo


def _forget_prep_kernel(x_ref, wfg_ref, bf_ref, fq_ref, fk_ref, qa_ref, ka_ref, carry_ref, *, tm):
    i = pl.program_id(0)

    @pl.when(i == 0)
    def _init():
        carry_ref[...] = jnp.zeros_like(carry_ref)

    w_fg = jnp.concatenate([wfg_ref[...], jnp.zeros((LANES - N_HEADS, wfg_ref.shape[1]), F32)], axis=0)
    z = lax.dot_general(x_ref[...], w_fg.astype(BF16), (((1,), (1,)), ((), ())),
                        preferred_element_type=F32) + bf_ref[...]
    log_f = -(jnp.maximum(-z, 0.0) + jnp.log1p(jnp.exp(-jnp.abs(z)))) * LOG2E
    row = lax.broadcasted_iota(jnp.int32, (tm, tm), 0)
    col = lax.broadcasted_iota(jnp.int32, (tm, tm), 1)
    tri = (col <= row).astype(BF16)
    hi, mid, lo = _split3(log_f)
    cum = (jnp.dot(tri, hi, preferred_element_type=F32) + jnp.dot(tri, mid, preferred_element_type=F32)
           + jnp.dot(tri, lo, preferred_element_type=F32)) + carry_ref[...]
    carry_ref[...] = cum[tm - 1:tm, :]
    c_hi, c_mid, c_lo = (p.astype(F32) for p in _split3(cum))
    lane = lax.broadcasted_iota(jnp.int32, (1, LANES), 1)
    for h in range(N_HEADS):
        ch, cm, cl = c_hi[:, h:h + 1], c_mid[:, h:h + 1], c_lo[:, h:h + 1]
        q_extra = jnp.where(lane == 0, ch, jnp.where(lane == 1, cm, jnp.where(lane == 2, cl,
                  jnp.where(lane < 6, 1.0, 0.0))))
        k_extra = jnp.where(lane < 3, 1.0, jnp.where(lane == 3, -ch, jnp.where(lane == 4, -cm,
                  jnp.where(lane == 5, -cl, 0.0))))
        q_aug = jnp.concatenate([fq_ref[:, h * HEAD_DIM:(h + 1) * HEAD_DIM].astype(F32), q_extra], axis=1)
        qa_ref[h] = q_aug.T.astype(BF16)
        ka_ref[h, :, 0:HEAD_DIM] = fk_ref[:, h * HEAD_DIM:(h + 1) * HEAD_DIM]
        ka_ref[h, :, HEAD_DIM:2 * HEAD_DIM] = k_extra.astype(BF16)


def _forget_prep(x_bf, w_in_t, b_fg, proj, layer, tm, tq):
    s, d = x_bf.shape
    per_tile = tq // tm
    return pl.pallas_call(
        functools.partial(_forget_prep_kernel, tm=tm),
        grid=(s // tm,),
        in_specs=[
            pl.BlockSpec((tm, d), lambda i: (i, 0)),
            pl.BlockSpec((None, N_HEADS, d), lambda i: (layer, N_MAIN // N_HEADS, 0)),
            pl.BlockSpec((1, LANES), lambda i: (0, 0)),
            pl.BlockSpec((tm, SEG), lambda i: (i, 0)),
            pl.BlockSpec((tm, SEG), lambda i: (i, 1)),
        ],
        out_specs=[pl.BlockSpec((N_HEADS, None, 2 * HEAD_DIM, tm),
                                lambda i: (0, i // per_tile, 0, i % per_tile)),
                   pl.BlockSpec((N_HEADS, tm, 2 * HEAD_DIM), lambda i: (0, i, 0))],
        out_shape=[jax.ShapeDtypeStruct((N_HEADS, s // tq, 2 * HEAD_DIM, tq), BF16),
                   jax.ShapeDtypeStruct((N_HEADS, s, 2 * HEAD_DIM), BF16)],
        scratch_shapes=[pltpu.VMEM((1, LANES), F32)],
        compiler_params=_cparams(("arbitrary",), 32),
        name="forget_prep",
    )(x_bf, w_in_t, b_fg, proj, proj)


def _transpose_kernel(fox_v_ref, diff_q_ref, diff_v_ref, o_ref, *, tk):
    for g, ref in enumerate((fox_v_ref, diff_q_ref, diff_v_ref)):
        xt = ref[...].astype(F32).T
        for c in range(o_ref.shape[1]):
            o_ref[g, c] = xt[:, c * tk:(c + 1) * tk].astype(BF16)


def _transpose_segs(plain, rope, tm, tk):
    s = plain.shape[0]
    seg_spec = lambda blk: pl.BlockSpec((tm, SEG), lambda i: (i, blk))
    return pl.pallas_call(
        functools.partial(_transpose_kernel, tk=tk),
        grid=(s // tm,),
        in_specs=[seg_spec(PLAIN_SEGS.index(2)), seg_spec(ROPE_SEGS.index(3)), seg_spec(PLAIN_SEGS.index(5))],
        out_specs=pl.BlockSpec((3, tm // tk, SEG, tk), lambda i: (0, i, 0, 0)),
        out_shape=jax.ShapeDtypeStruct((3, s // tk, SEG, tk), BF16),
        compiler_params=_cparams(("arbitrary",), 40),
        name="transpose_segs",
    )(plain, rope, plain)


ONES_ROWS = 16


class _AttnState:
    S_SLOTS = 4
    P_SLOTS = 2
    N_REFS = 2 * S_SLOTS + 2 * P_SLOTS + 2

    def __init__(self, refs):
        a, b = self.S_SLOTS, self.S_SLOTS + self.P_SLOTS
        self.s = refs[0:a]
        self.p = refs[a:b]
        self.alpha = refs[b:b + self.P_SLOTS]
        c = b + self.P_SLOTS
        self.chunk_max = refs[c:c + self.S_SLOTS]
        self.m, self.acc = refs[c + self.S_SLOTS:]

    @classmethod
    def scratch(cls, tq, tk):
        row = pltpu.VMEM((1, tq), F32)
        return ([pltpu.VMEM((tk, tq), F32)] * cls.S_SLOTS + [pltpu.VMEM((tk, tq), BF16)] * cls.P_SLOTS
                + [row] * (cls.P_SLOTS + cls.S_SLOTS) + [row, pltpu.VMEM((HEAD_DIM + ONES_ROWS, tq), F32)])

    def init(self):
        self.m[...] = jnp.full_like(self.m, -jnp.inf)
        self.acc[...] = jnp.zeros_like(self.acc)
        self.p[1][...] = jnp.zeros_like(self.p[1])
        self.alpha[1][...] = jnp.ones_like(self.alpha[1])

    def normalized(self):
        return self.acc[0:HEAD_DIM, :] / self.acc[HEAD_DIM:HEAD_DIM + 1, :]


def _attn_pipeline(i, q_t, k_ref, v_t_ref, states, tq, tk):
    n_sub = tq // tk
    assert tq == n_sub * tk and n_sub == _AttnState.S_SLOTS

    def scores(c, slot, q0=0):
        k = k_ref[pl.ds(pl.multiple_of(c * tk, tk), tk), :]
        for st, q in zip(states, q_t):
            s = jnp.dot(k, q[:, q0:], preferred_element_type=F32)
            st.s[slot][:, q0:] = s
            st.chunk_max[slot][:, q0:] = jnp.max(s, axis=0, keepdims=True)

    def softmax(s_slot, slot, k0=None, q0=0):
        for st in states:
            s = st.s[s_slot][:, q0:]
            if k0 is None:
                chunk_max = st.chunk_max[s_slot][:, q0:]
            else:
                key = lax.broadcasted_iota(jnp.int32, s.shape, 0) + k0
                query = lax.broadcasted_iota(jnp.int32, s.shape, 1) + (i * tq + q0)
                s = jnp.where(key <= query, s, -jnp.inf)
                chunk_max = jnp.max(s, axis=0, keepdims=True)
            m_prev = st.m[:, q0:]
            m_new = jnp.maximum(m_prev, chunk_max)
            st.m[:, q0:] = m_new
            st.alpha[slot][:, q0:] = jnp.exp2(m_prev - m_new)
            st.p[slot][:, q0:] = jnp.exp2(s - m_new).astype(BF16)

    def values(c, slot, q0=0):
        v_t = jnp.concatenate([v_t_ref[c], jnp.ones((ONES_ROWS, tk), BF16)], axis=0)
        for st in states:
            st.acc[:, q0:] = st.alpha[slot][:, q0:] * st.acc[:, q0:] + jnp.dot(
                v_t, st.p[slot][:, q0:], preferred_element_type=F32)

    for st in states:
        st.init()
    scores(0, 0)
    scores(1, 1)

    def body(block, carry):
        c = n_sub * block
        for u in range(n_sub):
            scores(c + u + 2, (u + 2) % n_sub)
            softmax(u, u % 2)
            values(jnp.maximum(c + u - 1, 0), (u + 1) % 2)
        return carry

    lax.fori_loop(0, i, body, 0)
    c = n_sub * i
    for u in range(n_sub):
        if u + 2 < n_sub:
            scores(c + u + 2, u + 2, q0=(u + 2) * tk)
        softmax(u, u % 2, k0=(c + u) * tk, q0=u * tk)
        values(jnp.maximum(c + u - 1, 0), (u + 1) % 2, q0=max(u - 1, 0) * tk)
    values(c + n_sub - 1, (n_sub - 1) % 2, q0=(n_sub - 1) * tk)


def _fox_attn_kernel(q_ref, k_ref, v_ref, o_ref, *scratch, tq, tk):
    st = _AttnState(scratch)
    _attn_pipeline(pl.program_id(1), [q_ref[...]], k_ref, v_ref, [st], tq, tk)
    o_ref[...] = st.normalized().T.astype(BF16)


def _fox_attn(q_aug_t, k_aug, segs_t, tq, tk):
    _, s, _ = k_aug.shape
    return pl.pallas_call(
        functools.partial(_fox_attn_kernel, tq=tq, tk=tk),
        grid=(N_HEADS, s // tq),
        in_specs=[
            pl.BlockSpec((None, None, 2 * HEAD_DIM, tq), lambda h, i: (h, i, 0, 0)),
            pl.BlockSpec((None, s, 2 * HEAD_DIM), lambda h, i: (h, 0, 0)),
            pl.BlockSpec((None, s // tk, HEAD_DIM, tk), lambda h, i: (0, 0, h, 0)),
        ],
        out_specs=pl.BlockSpec((tq, HEAD_DIM), lambda h, i: (i, h)),
        out_shape=jax.ShapeDtypeStruct((s, SEG), BF16),
        scratch_shapes=_AttnState.scratch(tq, tk),
        compiler_params=_cparams(("arbitrary", "arbitrary"), 40),
        name="fox_attn",
    )(q_aug_t, k_aug, segs_t)


def _diff_attn_kernel(q_ref, k_ref, v_ref, dl_ref, gain_ref, o_ref, *scratch, tq, tk, lambda_init):
    st1 = _AttnState(scratch[:_AttnState.N_REFS])
    st2 = _AttnState(scratch[_AttnState.N_REFS:])
    q_t = jnp.concatenate([q_ref[c] for c in range(tq // tk)], axis=1)
    channel = lax.broadcasted_iota(jnp.int32, (HEAD_DIM, 1), 0)
    q1 = jnp.where(channel < QK_DIM, q_t, jnp.zeros_like(q_t))
    q2 = jnp.where(channel >= QK_DIM, q_t, jnp.zeros_like(q_t))
    _attn_pipeline(pl.program_id(1), [q1, q2], k_ref, v_ref, [st1, st2], tq, tk)

    dl = dl_ref[...]
    lam = (jnp.exp(jnp.sum(dl[0:1, :] * dl[1:2, :], axis=1, keepdims=True))
           - jnp.exp(jnp.sum(dl[2:3, :] * dl[3:4, :], axis=1, keepdims=True)) + lambda_init)
    o = st1.normalized() - lam * st2.normalized()
    o = o * lax.rsqrt(jnp.mean(o * o, axis=0, keepdims=True) + LN_EPS)
    o_ref[...] = (o.T * gain_ref[...] * (1.0 - lambda_init)).astype(BF16)


def _diff_attn(rope, segs_t, diff_lambda, diff_gain, layer, lambda_init, tq, tk):
    s = rope.shape[0]
    k_block0 = ROPE_SEGS.index(4) * SEG // HEAD_DIM
    return pl.pallas_call(
        functools.partial(_diff_attn_kernel, tq=tq, tk=tk, lambda_init=lambda_init),
        grid=(N_HEADS, s // tq),
        in_specs=[
            pl.BlockSpec((None, tq // tk, HEAD_DIM, tk), lambda h, i: (1, i, h, 0)),
            pl.BlockSpec((s, HEAD_DIM), lambda h, i: (0, k_block0 + h)),
            pl.BlockSpec((None, s // tk, HEAD_DIM, tk), lambda h, i: (2, 0, h, 0)),
            pl.BlockSpec((None, 4, QK_DIM), lambda h, i: (layer, 0, 0)),
            pl.BlockSpec((None, 1, HEAD_DIM), lambda h, i: (layer, 0, 0)),
        ],
        out_specs=pl.BlockSpec((tq, HEAD_DIM), lambda h, i: (i, h)),
        out_shape=jax.ShapeDtypeStruct((s, SEG), BF16),
        scratch_shapes=_AttnState.scratch(tq, tk) * 2,
        compiler_params=_cparams(("arbitrary", "arbitrary"), 40),
        name="diff_attn",
    )(segs_t, rope, segs_t, diff_lambda, diff_gain)


def _pool_kernel(u_ref, halo_ref, wg_ref, sc_ref, o_ref, *, tm):
    i = pl.program_id(0)
    tokens_seen = (lax.broadcasted_iota(jnp.int32, (tm, 1), 0) + i * tm + 1).astype(F32)
    for g, window in enumerate(POOL_WINDOWS):
        cols = slice(g * POOL_GROUP_DIM, (g + 1) * POOL_GROUP_DIM)
        u = u_ref[:, cols].astype(F32)
        halo = halo_ref[:, cols].astype(F32)
        halo = jnp.where(i > 0, halo, jnp.zeros_like(halo))
        ext = jnp.concatenate([halo, u], axis=0)
        shift = 1
        while shift < window:
            ext = ext + pltpu.roll(ext, shift, 0)
            shift *= 2
        pooled = ext[POOL_HALO:, :] / jnp.minimum(tokens_seen, float(window))
        delta = (pooled - u).astype(BF16)
        y = jnp.dot(delta, wg_ref[g].astype(BF16), preferred_element_type=F32) * sc_ref[:, cols]
        o_ref[:, cols] = y.astype(BF16)


def _pool(plain, w_pool_group, pool_scale, layer, tm):
    s = plain.shape[0]
    n_groups = len(POOL_WINDOWS)
    halo_blocks_per_tile = tm // POOL_HALO
    blk = PLAIN_SEGS.index(6)
    return pl.pallas_call(
        functools.partial(_pool_kernel, tm=tm),
        grid=(s // tm,),
        in_specs=[
            pl.BlockSpec((tm, SEG), lambda i: (i, blk)),
            pl.BlockSpec((POOL_HALO, SEG), lambda i: (jnp.maximum(i * halo_blocks_per_tile - 1, 0), blk)),
            pl.BlockSpec((None, n_groups, POOL_GROUP_DIM, POOL_GROUP_DIM), lambda i: (layer, 0, 0, 0)),
            pl.BlockSpec((None, 1, SEG), lambda i: (layer, 0, 0)),
        ],
        out_specs=pl.BlockSpec((tm, SEG), lambda i: (i, 0)),
        out_shape=jax.ShapeDtypeStruct((s, SEG), BF16),
        compiler_params=_cparams(("arbitrary",), 32),
        name="pool",
    )(plain, plain, w_pool_group, pool_scale)


def _merge_kernel(ya_ref, yb_ref, yc_ref, wa_ref, wb_ref, wc_ref, ga_ref, gb_ref, gc_ref, o_ref,
                  wab_ref, wbb_ref, wcb_ref):
    i = pl.program_id(1)

    def emit(ws):
        h = ga_ref[...].astype(F32) * jnp.dot(ya_ref[...], ws[0], preferred_element_type=F32)
        h = h + gb_ref[...].astype(F32) * jnp.dot(yb_ref[...], ws[1], preferred_element_type=F32)
        h = h + gc_ref[...].astype(F32) * jnp.dot(yc_ref[...], ws[2], preferred_element_type=F32)
        o_ref[...] = h.astype(BF16)

    _with_bf16_weights(i == 0, i != 0, [wa_ref, wb_ref, wc_ref], [wab_ref, wbb_ref, wcb_ref], emit)


def _merge(y_a, y_b, y_c, w_a, w_b, w_c, gates, layer, tm, tn):
    s = y_a.shape[0]
    gate_blocks = D_MODEL // tn
    y_spec = pl.BlockSpec((tm, SEG), lambda j, i: (i, 0))
    w_spec = pl.BlockSpec((None, SEG, tn), lambda j, i: (layer, 0, j))
    gate_spec = lambda br: pl.BlockSpec((tm, tn), lambda j, i: (i, br * gate_blocks + j))
    return pl.pallas_call(
        _merge_kernel,
        grid=(D_MODEL // tn, s // tm),
        in_specs=[y_spec, y_spec, y_spec, w_spec, w_spec, w_spec, gate_spec(0), gate_spec(1), gate_spec(2)],
        out_specs=pl.BlockSpec((tm, tn), lambda j, i: (i, j)),
        out_shape=jax.ShapeDtypeStruct((s, D_MODEL), BF16),
        scratch_shapes=[pltpu.VMEM((SEG, tn), BF16)] * 3,
        compiler_params=_cparams(("arbitrary", "arbitrary"), 48),
        name="merge",
    )(y_a, y_b, y_c, w_a, w_b, w_c, gates, gates, gates)


def _layer_norm_rows(z, g, b):
    mu = jnp.mean(z, axis=1, keepdims=True)
    zc = z - mu
    var = jnp.mean(zc * zc, axis=1, keepdims=True)
    return zc * lax.rsqrt(var + LN_EPS) * g + b


def _matmul_ln_kernel(h_ref, w_ref, x_ref, g_ref, b_ref, xo_ref, xb_ref, wbf_ref, *, alpha):
    i = pl.program_id(0)

    def emit(ws):
        mix = jnp.dot(h_ref[...], ws[0], preferred_element_type=F32)
        y = _layer_norm_rows(alpha * x_ref[...] + mix, g_ref[...], b_ref[...])
        xo_ref[...] = y
        xb_ref[...] = y.astype(BF16)

    _with_bf16_weights(i == 0, i != 0, [w_ref], [wbf_ref], emit)


def _matmul_ln(h, w, x, g, b, layer, alpha, tm):
    s, k = h.shape
    n = w.shape[-1]
    row_spec = pl.BlockSpec((tm, n), lambda i: (i, 0))
    par_spec = pl.BlockSpec((None, 1, n), lambda i: (layer, 0, 0))
    return pl.pallas_call(
        functools.partial(_matmul_ln_kernel, alpha=alpha),
        grid=(s // tm,),
        in_specs=[pl.BlockSpec((tm, k), lambda i: (i, 0)),
                  pl.BlockSpec((None, k, n), lambda i: (layer, 0, 0), pipeline_mode=pl.Buffered(1)),
                  row_spec, par_spec, par_spec],
        out_specs=[row_spec, row_spec],
        out_shape=[jax.ShapeDtypeStruct((s, n), F32), jax.ShapeDtypeStruct((s, n), BF16)],
        scratch_shapes=[pltpu.VMEM((k, n), BF16)],
        compiler_params=_cparams(("arbitrary",), 56),
        name="outproj_ln",
    )(h, w, x, g, b)


def _add_ln_kernel(x_ref, f_ref, g_ref, b_ref, xo_ref, xb_ref, *, alpha):
    y = _layer_norm_rows(alpha * x_ref[...] + f_ref[...], g_ref[...], b_ref[...])
    xo_ref[...] = y
    xb_ref[...] = y.astype(BF16)


def _add_ln(x, f, g, b, layer, alpha, tm):
    s, d = x.shape
    row_spec = pl.BlockSpec((tm, d), lambda i: (i, 0))
    par_spec = pl.BlockSpec((None, 1, d), lambda i: (layer, 0, 0))
    return pl.pallas_call(
        functools.partial(_add_ln_kernel, alpha=alpha),
        grid=(s // tm,),
        in_specs=[row_spec, row_spec, par_spec, par_spec],
        out_specs=[row_spec, row_spec],
        out_shape=[jax.ShapeDtypeStruct((s, d), F32), jax.ShapeDtypeStruct((s, d), BF16)],
        compiler_params=_cparams(("arbitrary",), 32),
        name="add_ln",
    )(x, f, g, b)


def _grouped_weights(tf_ref, tg_ref, ge_ref, meta_ref, w_hbms, wbuf, wbf_refs, sem, emit, *, layer, tn):
    j = pl.program_id(0)
    t = pl.program_id(1)
    n_used, n_groups = meta_ref[0], meta_ref[1]
    active = t < n_used
    first = tf_ref[t] == 1
    g = tg_ref[t]
    block = j * n_groups + g
    slot = block % 2

    def copies(expert, col_block, sl):
        col = pl.multiple_of(col_block * tn, tn)
        return [pltpu.make_async_copy(w.at[layer, expert, :, pl.ds(col, tn)], wbuf.at[sl, m], sem.at[sl])
                for m, w in enumerate(w_hbms)]

    @pl.when(active & first)
    def _fresh():
        @pl.when(block == 0)
        def _cold_start():
            for cp in copies(ge_ref[0], 0, 0):
                cp.start()

        for cp in copies(ge_ref[g], j, slot):
            cp.wait()
        wrap = g + 1 == n_groups
        g_next = jnp.where(wrap, 0, g + 1)
        j_next = jnp.where(wrap, j + 1, j)

        @pl.when(j_next < pl.num_programs(0))
        def _prefetch():
            for cp in copies(ge_ref[g_next], j_next, 1 - slot):
                cp.start()

        ws = [wbuf[slot, m].astype(BF16) for m in range(len(w_hbms))]
        for dst, w in zip(wbf_refs, ws):
            dst[...] = w
        emit(ws)

    @pl.when(active & jnp.logical_not(first))
    def _cached():
        emit([r[...] for r in wbf_refs])

    return active


def _gate_up_kernel(tf_ref, tg_ref, ge_ref, meta_ref, x_ref, wg_hbm, wu_hbm, o_ref,
                    wbuf, wgb_ref, wub_ref, sem, *, layer, tn):
    def emit(ws):
        x = x_ref[...]
        g = jnp.dot(x, ws[0], preferred_element_type=F32)
        u = jnp.dot(x, ws[1], preferred_element_type=F32)
        o_ref[...] = (g * jax.nn.sigmoid(g) * u).astype(BF16)

    active = _grouped_weights(tf_ref, tg_ref, ge_ref, meta_ref, [wg_hbm, wu_hbm], wbuf, [wgb_ref, wub_ref], sem,
                              emit, layer=layer, tn=tn)

    @pl.when(jnp.logical_not(active))
    def _unused_tile():
        o_ref[...] = jnp.zeros_like(o_ref)


def _row_tile(j, t, tf, tg, ge, meta):
    return (jnp.minimum(t, meta[0] - 1), 0)


def _gate_up(groups, x_sorted, w_gate, w_up, layer, tm, tn):
    n_slots, k = x_sorted.shape
    n = w_gate.shape[-1]
    return pl.pallas_call(
        functools.partial(_gate_up_kernel, layer=layer, tn=tn),
        grid_spec=pltpu.PrefetchScalarGridSpec(
            num_scalar_prefetch=4,
            grid=(n // tn, n_slots // tm),
            in_specs=[pl.BlockSpec((tm, k), _row_tile), pl.BlockSpec(memory_space=pl.ANY),
                      pl.BlockSpec(memory_space=pl.ANY)],
            out_specs=pl.BlockSpec((tm, tn), lambda j, t, tf, tg, ge, meta: (t, j)),
            scratch_shapes=[pltpu.VMEM((2, 2, k, tn), F32), pltpu.VMEM((k, tn), BF16), pltpu.VMEM((k, tn), BF16),
                            pltpu.SemaphoreType.DMA((2,))],
        ),
        out_shape=jax.ShapeDtypeStruct((n_slots, n), BF16),
        compiler_params=_cparams(("arbitrary", "arbitrary"), 52),
        name="ffn_gate_up",
    )(*groups, x_sorted, w_gate, w_up)


def _down_kernel(tf_ref, tg_ref, ge_ref, meta_ref, h_ref, w_hbm, o_ref, wbuf, wbf_ref, sem, *, layer, tn):
    def emit(ws):
        o_ref[...] = jnp.dot(h_ref[...], ws[0], preferred_element_type=F32)

    active = _grouped_weights(tf_ref, tg_ref, ge_ref, meta_ref, [w_hbm], wbuf, [wbf_ref], sem, emit,
                              layer=layer, tn=tn)

    @pl.when(jnp.logical_not(active))
    def _unused_tile():
        o_ref[...] = jnp.zeros_like(o_ref)


def _down(groups, h_sorted, w_down, layer, tm, tn):
    n_slots, k = h_sorted.shape
    n = w_down.shape[-1]
    return pl.pallas_call(
        functools.partial(_down_kernel, layer=layer, tn=tn),
        grid_spec=pltpu.PrefetchScalarGridSpec(
            num_scalar_prefetch=4,
            grid=(n // tn, n_slots // tm),
            in_specs=[pl.BlockSpec((tm, k), _row_tile), pl.BlockSpec(memory_space=pl.ANY)],
            out_specs=pl.BlockSpec((tm, tn), lambda j, t, tf, tg, ge, meta: (t, j)),
            scratch_shapes=[pltpu.VMEM((2, 1, k, tn), F32), pltpu.VMEM((k, tn), BF16),
                            pltpu.SemaphoreType.DMA((2,))],
        ),
        out_shape=jax.ShapeDtypeStruct((n_slots, n), F32),
        compiler_params=_cparams(("arbitrary", "arbitrary"), 56),
        name="ffn_down",
    )(*groups, h_sorted, w_down)


def _single_group(n_tiles):
    return (jnp.zeros((n_tiles,), jnp.int32).at[0].set(1),
            jnp.zeros((n_tiles,), jnp.int32),
            jnp.zeros((N_EXPERTS,), jnp.int32),
            jnp.array([n_tiles, 1], jnp.int32))


def _router_kernel(x_ref, rw_ref, rb_ref, o_ref, cnt_ref, carry_ref, *, tm):
    i = pl.program_id(0)

    @pl.when(i == 0)
    def _init():
        carry_ref[...] = jnp.zeros_like(carry_ref)

    logits = jnp.dot(x_ref[...], rw_ref[...], preferred_element_type=F32,
                     precision=lax.Precision.HIGHEST) + rb_ref[...]
    lane = lax.broadcasted_iota(jnp.int32, (tm, LANES), 1).astype(F32)
    lg = jnp.where(lane < N_EXPERTS, logits, -jnp.inf)
    v1 = jnp.max(lg, axis=1, keepdims=True)
    e1 = jnp.min(jnp.where(lg == v1, lane, float(LANES)), axis=1, keepdims=True)
    lg2 = jnp.where(lane == e1, -jnp.inf, lg)
    v2 = jnp.max(lg2, axis=1, keepdims=True)
    e2 = jnp.min(jnp.where(lg2 == v2, lane, float(LANES)), axis=1, keepdims=True)
    t2 = jnp.exp(v2 - v1)
    w1 = 1.0 / (1.0 + t2)
    w2 = t2 / (1.0 + t2)
    pick1 = lane == e1
    pick2 = lane == e2
    onehot = jnp.where(pick1 | pick2, 1.0, 0.0)
    row = lax.broadcasted_iota(jnp.int32, (tm, tm), 0)
    col = lax.broadcasted_iota(jnp.int32, (tm, tm), 1)
    before = (col < row).astype(BF16)
    seen = jnp.dot(before, onehot.astype(BF16), preferred_element_type=F32) + carry_ref[...]
    r1 = jnp.sum(jnp.where(pick1, seen, 0.0), axis=1, keepdims=True)
    r2 = jnp.sum(jnp.where(pick2, seen, 0.0), axis=1, keepdims=True)
    carry_ref[...] = carry_ref[...] + jnp.sum(onehot, axis=0, keepdims=True)
    o_ref[...] = jnp.where(lane == 0, e1, jnp.where(lane == 1, e2, jnp.where(lane == 2, w1,
                 jnp.where(lane == 3, w2, jnp.where(lane == 4, r1, jnp.where(lane == 5, r2, 0.0))))))
    cnt_ref[...] = carry_ref[...]


def _router(x, rw_pad, rb_pad, tm):
    s, d = x.shape
    return pl.pallas_call(
        functools.partial(_router_kernel, tm=tm),
        grid=(s // tm,),
        in_specs=[pl.BlockSpec((tm, d), lambda i: (i, 0)),
                  pl.BlockSpec((d, LANES), lambda i: (0, 0)),
                  pl.BlockSpec((1, LANES), lambda i: (0, 0))],
        out_specs=[pl.BlockSpec((tm, LANES), lambda i: (i, 0)), pl.BlockSpec((1, LANES), lambda i: (0, 0))],
        out_shape=[jax.ShapeDtypeStruct((s, LANES), F32), jax.ShapeDtypeStruct((1, LANES), F32)],
        scratch_shapes=[pltpu.VMEM((1, LANES), F32)],
        compiler_params=_cparams(("arbitrary",), 32),
        name="router",
    )(x, rw_pad, rb_pad)


def _scatter_copies(x_ref, o_hbm, p1_ref, p2_ref, base, r, sem):
    return (pltpu.make_async_copy(x_ref.at[r], o_hbm.at[p1_ref[base + r]], sem),
            pltpu.make_async_copy(x_ref.at[r], o_hbm.at[p2_ref[base + r]], sem))


def _dispatch_kernel(p1_ref, p2_ref, x_ref, init_hbm, o_hbm, sem, *, tt):
    del init_hbm
    base = pl.program_id(0) * tt

    def issue(r, carry):
        for cp in _scatter_copies(x_ref, o_hbm, p1_ref, p2_ref, base, r, sem):
            cp.start()
        return carry

    lax.fori_loop(0, tt, issue, 0, unroll=8)

    def drain(r, carry):
        for cp in _scatter_copies(x_ref, o_hbm, p1_ref, p2_ref, base, r, sem):
            cp.wait()
        return carry

    lax.fori_loop(0, tt, drain, 0, unroll=8)


def _dispatch(pos1, pos2, x_bf, n_slots, tt):
    s, d = x_bf.shape
    x_slabs = x_bf.reshape(s, d // LANES, LANES)
    out = pl.pallas_call(
        functools.partial(_dispatch_kernel, tt=tt),
        grid_spec=pltpu.PrefetchScalarGridSpec(
            num_scalar_prefetch=2,
            grid=(s // tt,),
            in_specs=[pl.BlockSpec((tt, d // LANES, LANES), lambda i, p1, p2: (i, 0, 0)),
                      pl.BlockSpec(memory_space=pl.ANY)],
            out_specs=pl.BlockSpec(memory_space=pl.ANY),
            scratch_shapes=[pltpu.SemaphoreType.DMA],
        ),
        out_shape=jax.ShapeDtypeStruct((n_slots, d // LANES, LANES), BF16),
        input_output_aliases={3: 0},
        compiler_params=_cparams(("arbitrary",), 32),
        name="moe_dispatch",
    )(pos1, pos2, x_slabs, jnp.zeros((n_slots, d // LANES, LANES), BF16))
    return out.reshape(n_slots, d)


def _row_copy(src_hbm, src_row, buf, dst_row, sem):
    return pltpu.make_async_copy(src_hbm.at[pl.ds(src_row, 1), :], buf.at[pl.ds(dst_row, 1), :], sem)


def _combine_ln_kernel(p1_ref, p2_ref, y_hbm, x_ref, route_ref, g_ref, b_ref, xo_ref, xb_ref,
                       buf, sem, *, tc, alpha):
    i = pl.program_id(0)

    def copies(step, slot, r):
        t = step * tc + r
        return (_row_copy(y_hbm, p1_ref[t], buf.at[slot, 0], r, sem.at[slot]),
                _row_copy(y_hbm, p2_ref[t], buf.at[slot, 1], r, sem.at[slot]))

    def issue(step, slot):
        def body(r, carry):
            for cp in copies(step, slot, r):
                cp.start()
            return carry

        lax.fori_loop(0, tc, body, 0, unroll=8)

    @pl.when(i == 0)
    def _first():
        issue(0, 0)

    @pl.when(i + 1 < pl.num_programs(0))
    def _prefetch():
        issue(i + 1, (i + 1) % 2)

    slot = i % 2

    def drain(r, carry):
        for cp in copies(i, slot, r):
            cp.wait()
        return carry

    lax.fori_loop(0, tc, drain, 0, unroll=8)
    route = route_ref[...]
    f = route[:, 2:3] * buf[slot, 0] + route[:, 3:4] * buf[slot, 1]
    y = _layer_norm_rows(alpha * x_ref[...] + f, g_ref[...], b_ref[...])
    xo_ref[...] = y
    xb_ref[...] = y.astype(BF16)


def _combine_ln(pos1, pos2, y_sorted, x, route, g, b, layer, alpha, tc):
    s, d = x.shape
    row_spec = pl.BlockSpec((tc, d), lambda i, p1, p2: (i, 0))
    par_spec = pl.BlockSpec((None, 1, d), lambda i, p1, p2: (layer, 0, 0))
    return pl.pallas_call(
        functools.partial(_combine_ln_kernel, tc=tc, alpha=alpha),
        grid_spec=pltpu.PrefetchScalarGridSpec(
            num_scalar_prefetch=2,
            grid=(s // tc,),
            in_specs=[pl.BlockSpec(memory_space=pl.ANY), row_spec,
                      pl.BlockSpec((tc, LANES), lambda i, p1, p2: (i, 0)), par_spec, par_spec],
            out_specs=[row_spec, row_spec],
            scratch_shapes=[pltpu.VMEM((2, 2, tc, d), F32), pltpu.SemaphoreType.DMA((2,))],
        ),
        out_shape=[jax.ShapeDtypeStruct((s, d), F32), jax.ShapeDtypeStruct((s, d), BF16)],
        compiler_params=_cparams(("arbitrary",), 32),
        name="moe_combine_ln",
    )(pos1, pos2, y_sorted, x, route, g, b)


def _moe_plan(route, counts, tm, n_tiles):
    expert = route[:, 0:2].astype(jnp.int32)
    rank = route[:, 4:6].astype(jnp.int32)
    count = counts[0, :N_EXPERTS].astype(jnp.int32)
    padded = ((count + tm - 1) // tm) * tm
    ends = jnp.cumsum(padded)
    starts = ends - padded
    pos = starts[expert] + rank
    n_used = ends[-1] // tm
    tile = jnp.arange(n_tiles, dtype=jnp.int32)
    last = jnp.minimum(tile, n_used - 1)
    tile_expert = jnp.sum(last[:, None] * tm >= ends[None, :], axis=1).astype(jnp.int32)
    prev = jnp.concatenate([jnp.full((1,), -1, jnp.int32), tile_expert[:-1]])
    tile_first = (tile_expert != prev).astype(jnp.int32)
    nonempty = count > 0
    group_of_expert = jnp.cumsum(nonempty.astype(jnp.int32)) - 1
    experts = jnp.arange(N_EXPERTS, dtype=jnp.int32)
    group_expert = jnp.argsort(jnp.where(nonempty, experts, experts + N_EXPERTS)).astype(jnp.int32)
    meta = jnp.stack([n_used, jnp.sum(nonempty)]).astype(jnp.int32)
    return pos, (tile_first, group_of_expert[tile_expert], group_expert, meta)


def _rope_tables(seq):
    pos = jnp.arange(seq, dtype=F32)
    inv = ROPE_THETA ** (-jnp.arange(0, QK_DIM, 2, dtype=F32) / QK_DIM)
    ang = pos[:, None] * inv[None, :]
    cos, sin = jnp.cos(ang), jnp.sin(ang)
    return jnp.tile(cos, (1, 4)), jnp.tile(jnp.concatenate([-sin, sin], axis=1), (1, 2))


def _pad_lanes(a):
    return jnp.pad(a, ((0, 0), (0, LANES - a.shape[1])))


def kernel(x, w_in, b_forget, w_pool_group, pool_scale, diff_lambda, diff_norm_gain, w_branch_a, w_branch_b, w_branch_c, b_gate, w_out, ln1_g, ln1_b, ln2_g, ln2_b, ffn_w_gate, ffn_w_up, ffn_w_down, router_w, router_b, expert_w_gate, expert_w_up, expert_w_down):
    batch, s, d = x.shape
    assert batch == 1 and d == D_MODEL and w_in.shape[-1] == N_MAIN + N_HEADS
    depth = w_in.shape[0]
    alpha = (2 * depth) ** 0.25

    tm = min(1024, s)
    tq = min(1024, s)
    tk = tq // 4
    te = min(512, s)
    tg = min(256, s)
    n_ff = 512
    n_tiles = 2 * s // te + N_EXPERTS
    n_slots = n_tiles * te

    cos_t, sin_t = _rope_tables(s)
    w_in_t = jnp.swapaxes(w_in, 1, 2)
    b_gate_flat = b_gate.reshape(depth, 1, 3 * D_MODEL)
    pool_scale3 = pool_scale.reshape(depth, 1, SEG)
    gain3 = diff_norm_gain.reshape(depth, 1, HEAD_DIM)
    ln = [p.reshape(depth, 1, D_MODEL) for p in (ln1_g, ln1_b, ln2_g, ln2_b)]
    dense_w = [w.reshape(w.shape[0], 1, *w.shape[1:]) for w in (ffn_w_gate, ffn_w_up, ffn_w_down)]

    xf = x.reshape(s, d)
    xb = xf.astype(BF16)
    for layer in range(depth):
        lambda_init = 0.8 - 0.6 * math.exp(-0.3 * layer)
        plain, rope, gates = _inproj_all(xb, w_in_t, b_gate_flat, cos_t, sin_t, layer, tm)
        b_fg = _pad_lanes(b_forget[layer].reshape(1, N_HEADS))
        q_aug_t, k_aug = _forget_prep(xb, w_in_t, b_fg, plain, layer, te, tq)
        segs_t = _transpose_segs(plain, rope, te, tk)
        y_a = _fox_attn(q_aug_t, k_aug, segs_t, tq, tk)
        y_b = _pool(plain, w_pool_group, pool_scale3, layer, tm)
        y_c = _diff_attn(rope, segs_t, diff_lambda, gain3, layer, lambda_init, tq, tk)
        h = _merge(y_a, y_b, y_c, w_branch_a, w_branch_b, w_branch_c, gates, layer, tm, 512)
        xf, xb = _matmul_ln(h, w_out, xf, ln[0], ln[1], layer, alpha, tg)

        j = layer // 2
        if layer % 2 == 0:
            hidden = _gate_up(_single_group(s // tm), xb, dense_w[0], dense_w[1], j, tm, n_ff)
            f = _down(_single_group(s // te), hidden, dense_w[2], j, te, 512)
            xf, xb = _add_ln(xf, f, ln[2], ln[3], layer, alpha, tg)
        else:
            route, counts = _router(xf, _pad_lanes(router_w[j]), _pad_lanes(router_b[j].reshape(1, N_EXPERTS)), te)
            pos, groups = _moe_plan(route, counts, te, n_tiles)
            x_sorted = _dispatch(pos[:, 0], pos[:, 1], xb, n_slots, te)
            hidden = _gate_up(groups, x_sorted, expert_w_gate, expert_w_up, j, te, n_ff)
            y_sorted = _down(groups, hidden, expert_w_down, j, te, 512)
            xf, xb = _combine_ln(pos[:, 0], pos[:, 1], y_sorted, xf, route, ln[2], ln[3], layer, alpha, tg)
    return xf.reshape(batch, s, d)
```

```python
import functools
import math

import jax
import jax.numpy as jnp
from jax import lax
from jax.experimental import pallas as pl
from jax.experimental.pallas import tpu as pltpu

BF16 = jnp.bfloat16
F32 = jnp.float32

D_MODEL = 2048
N_HEADS = 8
HEAD_DIM = 128
QK_DIM = 64
SEG = 1024
N_MAIN = 13 * SEG
POOL_WINDOWS = (2, 4, 8, 16)
POOL_GROUP_DIM = 256
POOL_HALO = 16
N_EXPERTS = 8
ROPE_THETA = 10000.0
LN_EPS = 1e-5
LOG2E = math.log2(math.e)
FOX_Q_SCALE = HEAD_DIM ** -0.5 * LOG2E
DIFF_Q_SCALE = QK_DIM ** -0.5 * LOG2E
LANES = 128

MIB = 2 ** 20


def _cparams(semantics, vmem_mib):
    return pltpu.CompilerParams(dimension_semantics=semantics, vmem_limit_bytes=vmem_mib * MIB)


def _with_bf16_weights(fresh, cached, w_refs, wbf_refs, emit, transpose=False):
    @pl.when(fresh)
    def _fresh():
        ws = [(w[...].T if transpose else w[...]).astype(BF16) for w in w_refs]
        for dst, w in zip(wbf_refs, ws):
            dst[...] = w
        emit(ws)

    @pl.when(cached)
    def _cached():
        emit([r[...] for r in wbf_refs])


def _rope_store(acc, cos, sin, scale, o_ref):
    lane = lax.broadcasted_iota(jnp.int32, (1, LANES), 1)
    first_half = (lane % QK_DIM) < (QK_DIM // 2)
    for c in range(SEG // LANES):
        a = acc[:, c * LANES:(c + 1) * LANES]
        partner = jnp.where(first_half, pltpu.roll(a, LANES - QK_DIM // 2, 1), pltpu.roll(a, QK_DIM // 2, 1))
        o_ref[:, c * LANES:(c + 1) * LANES] = ((a * cos + partner * sin) * scale).astype(BF16)


def _inproj_kernel(x_ref, w_ref, *rest, mode, first_scale):
    o_ref, wbf_ref = rest[-2:]
    j = pl.program_id(0)
    i = pl.program_id(1)

    def emit(ws):
        acc = jnp.dot(x_ref[...], ws[0], preferred_element_type=F32)
        scale = jnp.where(j == 0, first_scale, 1.0).astype(F32)
        if mode == "plain":
            o_ref[...] = (acc * scale).astype(BF16)
        elif mode == "rope":
            _rope_store(acc, rest[0][...], rest[1][...], scale, o_ref)
        else:
            o_ref[...] = jax.nn.sigmoid(acc + rest[0][...]).astype(BF16)

    _with_bf16_weights(i == 0, i != 0, [w_ref], [wbf_ref], emit, transpose=True)


PLAIN_SEGS = (0, 1, 2, 5, 6)
ROPE_SEGS = (3, 4)
GATE_SEGS = (7, 8, 9, 10, 11, 12)


def _inproj(x_bf, w_in_t, extras, extra_specs, layer, tm, mode, segs, first_scale):
    s, d = x_bf.shape
    jump = next((k for k in range(1, len(segs)) if segs[k] != segs[k - 1] + 1), len(segs))
    gap = segs[jump] - segs[jump - 1] - 1 if jump < len(segs) else 0
    seg_of = lambda j: segs[0] + j + jnp.where(j >= jump, gap, 0)
    return pl.pallas_call(
        functools.partial(_inproj_kernel, mode=mode, first_scale=first_scale),
        grid=(len(segs), s // tm),
        in_specs=[pl.BlockSpec((tm, d), lambda j, i: (i, 0)),
                  pl.BlockSpec((None, SEG, d), lambda j, i: (layer, seg_of(j), 0))] + extra_specs,
        out_specs=pl.BlockSpec((tm, SEG), lambda j, i: (i, j)),
        out_shape=jax.ShapeDtypeStruct((s, len(segs) * SEG), BF16),
        scratch_shapes=[pltpu.VMEM((d, SEG), BF16)],
        compiler_params=_cparams(("arbitrary", "arbitrary"), 48),
        name="inproj_" + mode,
    )(x_bf, w_in_t, *extras)


def _inproj_all(x_bf, w_in_t, b_gate_flat, cos_t, sin_t, layer, tm):
    table_spec = pl.BlockSpec((tm, LANES), lambda j, i: (i, 0))
    plain = _inproj(x_bf, w_in_t, [], [], layer, tm, "plain", PLAIN_SEGS, FOX_Q_SCALE)
    rope = _inproj(x_bf, w_in_t, [cos_t, sin_t], [table_spec, table_spec], layer, tm, "rope", ROPE_SEGS,
                   DIFF_Q_SCALE)
    gates = _inproj(x_bf, w_in_t, [b_gate_flat], [pl.BlockSpec((None, 1, SEG), lambda j, i: (layer, 0, j))],
                    layer, tm, "gate", GATE_SEGS, 1.0)
    return plain, rope, gates


def _split3(v):
    hi = v.astype(BF16)
    r = v - hi.astype(F32)
    mid = r.astype(BF16)
    lo = (r - mid.astype(F32)).astype(BF16)
    return hi, mid, lo


def _forget_prep_kernel(x_ref, wfg_ref, bf_ref, fq_ref, fk_ref, qa_ref, ka_ref, carry_ref, *, tm):
    i = pl.program_id(0)

    @pl.when(i == 0)
    def _init():
        carry_ref[...] = jnp.zeros_like(carry_ref)

    w_fg = jnp.concatenate([wfg_ref[...], jnp.zeros((LANES - N_HEADS, wfg_ref.shape[1]), F32)], axis=0)
    z = lax.dot_general(x_ref[...], w_fg.astype(BF16), (((1,), (1,)), ((), ())),
                        preferred_element_type=F32) + bf_ref[...]
    log_f = -(jnp.maximum(-z, 0.0) + jnp.log1p(jnp.exp(-jnp.abs(z)))) * LOG2E
    row = lax.broadcasted_iota(jnp.int32, (tm, tm), 0)
    col = lax.broadcasted_iota(jnp.int32, (tm, tm), 1)
    tri = (col <= row).astype(BF16)
    hi, mid, lo = _split3(log_f)
    cum = (jnp.dot(tri, hi, preferred_element_type=F32) + jnp.dot(tri, mid, preferred_element_type=F32)
           + jnp.dot(tri, lo, preferred_element_type=F32)) + carry_ref[...]
    carry_ref[...] = cum[tm - 1:tm, :]
    c_hi, c_mid, c_lo = (p.astype(F32) for p in _split3(cum))
    lane = lax.broadcasted_iota(jnp.int32, (1, LANES), 1)
    for h in range(N_HEADS):
        ch, cm, cl = c_hi[:, h:h + 1], c_mid[:, h:h + 1], c_lo[:, h:h + 1]
        q_extra = jnp.where(lane == 0, ch, jnp.where(lane == 1, cm, jnp.where(lane == 2, cl,
                  jnp.where(lane < 6, 1.0, 0.0))))
        k_extra = jnp.where(lane < 3, 1.0, jnp.where(lane == 3, -ch, jnp.where(lane == 4, -cm,
                  jnp.where(lane == 5, -cl, 0.0))))
        q_aug = jnp.concatenate([fq_ref[:, h * HEAD_DIM:(h + 1) * HEAD_DIM].astype(F32), q_extra], axis=1)
        qa_ref[h] = q_aug.T.astype(BF16)
        ka_ref[h, :, 0:HEAD_DIM] = fk_ref[:, h * HEAD_DIM:(h + 1) * HEAD_DIM]
        ka_ref[h, :, HEAD_DIM:2 * HEAD_DIM] = k_extra.astype(BF16)


def _forget_prep(x_bf, w_in_t, b_fg, proj, layer, tm, tq):
    s, d = x_bf.shape
    per_tile = tq // tm
    return pl.pallas_call(
        functools.partial(_forget_prep_kernel, tm=tm),
        grid=(s // tm,),
        in_specs=[
            pl.BlockSpec((tm, d), lambda i: (i, 0)),
            pl.BlockSpec((None, N_HEADS, d), lambda i: (layer, N_MAIN // N_HEADS, 0)),
            pl.BlockSpec((1, LANES), lambda i: (0, 0)),
            pl.BlockSpec((tm, SEG), lambda i: (i, 0)),
            pl.BlockSpec((tm, SEG), lambda i: (i, 1)),
        ],
        out_specs=[pl.BlockSpec((N_HEADS, None, 2 * HEAD_DIM, tm),
                                lambda i: (0, i // per_tile, 0, i % per_tile)),
                   pl.BlockSpec((N_HEADS, tm, 2 * HEAD_DIM), lambda i: (0, i, 0))],
        out_shape=[jax.ShapeDtypeStruct((N_HEADS, s // tq, 2 * HEAD_DIM, tq), BF16),
                   jax.ShapeDtypeStruct((N_HEADS, s, 2 * HEAD_DIM), BF16)],
        scratch_shapes=[pltpu.VMEM((1, LANES), F32)],
        compiler_params=_cparams(("arbitrary",), 32),
        name="forget_prep",
    )(x_bf, w_in_t, b_fg, proj, proj)


def _transpose_kernel(fox_v_ref, diff_q_ref, diff_v_ref, o_ref, *, tk):
    for g, ref in enumerate((fox_v_ref, diff_q_ref, diff_v_ref)):
        xt = ref[...].astype(F32).T
        for c in range(o_ref.shape[1]):
            o_ref[g, c] = xt[:, c * tk:(c + 1) * tk].astype(BF16)


def _transpose_segs(plain, rope, tm, tk):
    s = plain.shape[0]
    seg_spec = lambda blk: pl.BlockSpec((tm, SEG), lambda i: (i, blk))
    return pl.pallas_call(
        functools.partial(_transpose_kernel, tk=tk),
        grid=(s // tm,),
        in_specs=[seg_spec(PLAIN_SEGS.index(2)), seg_spec(ROPE_SEGS.index(3)), seg_spec(PLAIN_SEGS.index(5))],
        out_specs=pl.BlockSpec((3, tm // tk, SEG, tk), lambda i: (0, i, 0, 0)),
        out_shape=jax.ShapeDtypeStruct((3, s // tk, SEG, tk), BF16),
        compiler_params=_cparams(("arbitrary",), 40),
        name="transpose_segs",
    )(plain, rope, plain)


ONES_ROWS = 16


class _AttnState:
    S_SLOTS = 4
    P_SLOTS = 2
    N_REFS = 2 * S_SLOTS + 2 * P_SLOTS + 2

    def __init__(self, refs):
        a, b = self.S_SLOTS, self.S_SLOTS + self.P_SLOTS
        self.s = refs[0:a]
        self.p = refs[a:b]
        self.alpha = refs[b:b + self.P_SLOTS]
        c = b + self.P_SLOTS
        self.chunk_max = refs[c:c + self.S_SLOTS]
        self.m, self.acc = refs[c + self.S_SLOTS:]

    @classmethod
    def scratch(cls, tq, tk):
        row = pltpu.VMEM((1, tq), F32)
        return ([pltpu.VMEM((tk, tq), F32)] * cls.S_SLOTS + [pltpu.VMEM((tk, tq), BF16)] * cls.P_SLOTS
                + [row] * (cls.P_SLOTS + cls.S_SLOTS) + [row, pltpu.VMEM((HEAD_DIM + ONES_ROWS, tq), F32)])

    def init(self):
        self.m[...] = jnp.full_like(self.m, -jnp.inf)
        self.acc[...] = jnp.zeros_like(self.acc)
        self.p[1][...] = jnp.zeros_like(self.p[1])
        self.alpha[1][...] = jnp.ones_like(self.alpha[1])

    def normalized(self):
        return self.acc[0:HEAD_DIM, :] / self.acc[HEAD_DIM:HEAD_DIM + 1, :]


def _attn_pipeline(i, q_t, k_ref, v_t_ref, states, tq, tk):
    n_sub = tq // tk
    assert tq == n_sub * tk and n_sub == _AttnState.S_SLOTS

    def scores(c, slot, q0=0):
        k = k_ref[pl.ds(pl.multiple_of(c * tk, tk), tk), :]
        for st, q in zip(states, q_t):
            s = jnp.dot(k, q[:, q0:], preferred_element_type=F32)
            st.s[slot][:, q0:] = s
            st.chunk_max[slot][:, q0:] = jnp.max(s, axis=0, keepdims=True)

    def softmax(s_slot, slot, k0=None, q0=0):
        for st in states:
            s = st.s[s_slot][:, q0:]
            if k0 is None:
                chunk_max = st.chunk_max[s_slot][:, q0:]
            else:
                key = lax.broadcasted_iota(jnp.int32, s.shape, 0) + k0
                query = lax.broadcasted_iota(jnp.int32, s.shape, 1) + (i * tq + q0)
                s = jnp.where(key <= query, s, -jnp.inf)
                chunk_max = jnp.max(s, axis=0, keepdims=True)
            m_prev = st.m[:, q0:]
            m_new = jnp.maximum(m_prev, chunk_max)
            st.m[:, q0:] = m_new
            st.alpha[slot][:, q0:] = jnp.exp2(m_prev - m_new)
            st.p[slot][:, q0:] = jnp.exp2(s - m_new).astype(BF16)

    def values(c, slot, q0=0):
        v_t = jnp.concatenate([v_t_ref[c], jnp.ones((ONES_ROWS, tk), BF16)], axis=0)
        for st in states:
            st.acc[:, q0:] = st.alpha[slot][:, q0:] * st.acc[:, q0:] + jnp.dot(
                v_t, st.p[slot][:, q0:], preferred_element_type=F32)

    for st in states:
        st.init()
    scores(0, 0)
    scores(1, 1)

    def body(block, carry):
        c = n_sub * block
        for u in range(n_sub):
            scores(c + u + 2, (u + 2) % n_sub)
            softmax(u, u % 2)
            values(jnp.maximum(c + u - 1, 0), (u + 1) % 2)
        return carry

    lax.fori_loop(0, i, body, 0)
    c = n_sub * i
    for u in range(n_sub):
        if u + 2 < n_sub:
            scores(c + u + 2, u + 2, q0=(u + 2) * tk)
        softmax(u, u % 2, k0=(c + u) * tk, q0=u * tk)
        values(jnp.maximum(c + u - 1, 0), (u + 1) % 2, q0=max(u - 1, 0) * tk)
    values(c + n_sub - 1, (n_sub - 1) % 2, q0=(n_sub - 1) * tk)


def _fox_attn_kernel(q_ref, k_ref, v_ref, o_ref, *scratch, tq, tk):
    st = _AttnState(scratch)
    _attn_pipeline(pl.program_id(1), [q_ref[...]], k_ref, v_ref, [st], tq, tk)
    o_ref[...] = st.normalized().T.astype(BF16)


def _fox_attn(q_aug_t, k_aug, segs_t, tq, tk):
    _, s, _ = k_aug.shape
    return pl.pallas_call(
        functools.partial(_fox_attn_kernel, tq=tq, tk=tk),
        grid=(N_HEADS, s // tq),
        in_specs=[
            pl.BlockSpec((None, None, 2 * HEAD_DIM, tq), lambda h, i: (h, i, 0, 0)),
            pl.BlockSpec((None, s, 2 * HEAD_DIM), lambda h, i: (h, 0, 0)),
            pl.BlockSpec((None, s // tk, HEAD_DIM, tk), lambda h, i: (0, 0, h, 0)),
        ],
        out_specs=pl.BlockSpec((tq, HEAD_DIM), lambda h, i: (i, h)),
        out_shape=jax.ShapeDtypeStruct((s, SEG), BF16),
        scratch_shapes=_AttnState.scratch(tq, tk),
        compiler_params=_cparams(("arbitrary", "arbitrary"), 40),
        name="fox_attn",
    )(q_aug_t, k_aug, segs_t)


def _diff_attn_kernel(q_ref, k_ref, v_ref, dl_ref, gain_ref, o_ref, *scratch, tq, tk, lambda_init):
    st1 = _AttnState(scratch[:_AttnState.N_REFS])
    st2 = _AttnState(scratch[_AttnState.N_REFS:])
    q_t = jnp.concatenate([q_ref[c] for c in range(tq // tk)], axis=1)
    channel = lax.broadcasted_iota(jnp.int32, (HEAD_DIM, 1), 0)
    q1 = jnp.where(channel < QK_DIM, q_t, jnp.zeros_like(q_t))
    q2 = jnp.where(channel >= QK_DIM, q_t, jnp.zeros_like(q_t))
    _attn_pipeline(pl.program_id(1), [q1, q2], k_ref, v_ref, [st1, st2], tq, tk)

    dl = dl_ref[...]
    lam = (jnp.exp(jnp.sum(dl[0:1, :] * dl[1:2, :], axis=1, keepdims=True))
           - jnp.exp(jnp.sum(dl[2:3, :] * dl[3:4, :], axis=1, keepdims=True)) + lambda_init)
    o = st1.normalized() - lam * st2.normalized()
    o = o * lax.rsqrt(jnp.mean(o * o, axis=0, keepdims=True) + LN_EPS)
    o_ref[...] = (o.T * gain_ref[...] * (1.0 - lambda_init)).astype(BF16)


def _diff_attn(rope, segs_t, diff_lambda, diff_gain, layer, lambda_init, tq, tk):
    s = rope.shape[0]
    k_block0 = ROPE_SEGS.index(4) * SEG // HEAD_DIM
    return pl.pallas_call(
        functools.partial(_diff_attn_kernel, tq=tq, tk=tk, lambda_init=lambda_init),
        grid=(N_HEADS, s // tq),
        in_specs=[
            pl.BlockSpec((None, tq // tk, HEAD_DIM, tk), lambda h, i: (1, i, h, 0)),
            pl.BlockSpec((s, HEAD_DIM), lambda h, i: (0, k_block0 + h)),
            pl.BlockSpec((None, s // tk, HEAD_DIM, tk), lambda h, i: (2, 0, h, 0)),
            pl.BlockSpec((None, 4, QK_DIM), lambda h, i: (layer, 0, 0)),
            pl.BlockSpec((None, 1, HEAD_DIM), lambda h, i: (layer, 0, 0)),
        ],
        out_specs=pl.BlockSpec((tq, HEAD_DIM), lambda h, i: (i, h)),
        out_shape=jax.ShapeDtypeStruct((s, SEG), BF16),
        scratch_shapes=_AttnState.scratch(tq, tk) * 2,
        compiler_params=_cparams(("arbitrary", "arbitrary"), 40),
        name="diff_attn",
    )(segs_t, rope, segs_t, diff_lambda, diff_gain)


def _pool_kernel(u_ref, halo_ref, wg_ref, sc_ref, o_ref, *, tm):
    i = pl.program_id(0)
    tokens_seen = (lax.broadcasted_iota(jnp.int32, (tm, 1), 0) + i * tm + 1).astype(F32)
    for g, window in enumerate(POOL_WINDOWS):
        cols = slice(g * POOL_GROUP_DIM, (g + 1) * POOL_GROUP_DIM)
        u = u_ref[:, cols].astype(F32)
        halo = halo_ref[:, cols].astype(F32)
        halo = jnp.where(i > 0, halo, jnp.zeros_like(halo))
        ext = jnp.concatenate([halo, u], axis=0)
        shift = 1
        while shift < window:
            ext = ext + pltpu.roll(ext, shift, 0)
            shift *= 2
        pooled = ext[POOL_HALO:, :] / jnp.minimum(tokens_seen, float(window))
        delta = (pooled - u).astype(BF16)
        y = jnp.dot(delta, wg_ref[g].astype(BF16), preferred_element_type=F32) * sc_ref[:, cols]
        o_ref[:, cols] = y.astype(BF16)


def _pool(plain, w_pool_group, pool_scale, layer, tm):
    s = plain.shape[0]
    n_groups = len(POOL_WINDOWS)
    halo_blocks_per_tile = tm // POOL_HALO
    blk = PLAIN_SEGS.index(6)
    return pl.pallas_call(
        functools.partial(_pool_kernel, tm=tm),
        grid=(s // tm,),
        in_specs=[
            pl.BlockSpec((tm, SEG), lambda i: (i, blk)),
            pl.BlockSpec((POOL_HALO, SEG), lambda i: (jnp.maximum(i * halo_blocks_per_tile - 1, 0), blk)),
            pl.BlockSpec((None, n_groups, POOL_GROUP_DIM, POOL_GROUP_DIM), lambda i: (layer, 0, 0, 0)),
            pl.BlockSpec((None, 1, SEG), lambda i: (layer, 0, 0)),
        ],
        out_specs=pl.BlockSpec((tm, SEG), lambda i: (i, 0)),
        out_shape=jax.ShapeDtypeStruct((s, SEG), BF16),
        compiler_params=_cparams(("arbitrary",), 32),
        name="pool",
    )(plain, plain, w_pool_group, pool_scale)


def _merge_kernel(ya_ref, yb_ref, yc_ref, wa_ref, wb_ref, wc_ref, ga_ref, gb_ref, gc_ref, o_ref,
                  wab_ref, wbb_ref, wcb_ref):
    i = pl.program_id(1)

    def emit(ws):
        h = ga_ref[...].astype(F32) * jnp.dot(ya_ref[...], ws[0], preferred_element_type=F32)
        h = h + gb_ref[...].astype(F32) * jnp.dot(yb_ref[...], ws[1], preferred_element_type=F32)
        h = h + gc_ref[...].astype(F32) * jnp.dot(yc_ref[...], ws[2], preferred_element_type=F32)
        o_ref[...] = h.astype(BF16)

    _with_bf16_weights(i == 0, i != 0, [wa_ref, wb_ref, wc_ref], [wab_ref, wbb_ref, wcb_ref], emit)


def _merge(y_a, y_b, y_c, w_a, w_b, w_c, gates, layer, tm, tn):
    s = y_a.shape[0]
    gate_blocks = D_MODEL // tn
    y_spec = pl.BlockSpec((tm, SEG), lambda j, i: (i, 0))
    w_spec = pl.BlockSpec((None, SEG, tn), lambda j, i: (layer, 0, j))
    gate_spec = lambda br: pl.BlockSpec((tm, tn), lambda j, i: (i, br * gate_blocks + j))
    return pl.pallas_call(
        _merge_kernel,
        grid=(D_MODEL // tn, s // tm),
        in_specs=[y_spec, y_spec, y_spec, w_spec, w_spec, w_spec, gate_spec(0), gate_spec(1), gate_spec(2)],
        out_specs=pl.BlockSpec((tm, tn), lambda j, i: (i, j)),
        out_shape=jax.ShapeDtypeStruct((s, D_MODEL), BF16),
        scratch_shapes=[pltpu.VMEM((SEG, tn), BF16)] * 3,
        compiler_params=_cparams(("arbitrary", "arbitrary"), 48),
        name="merge",
    )(y_a, y_b, y_c, w_a, w_b, w_c, gates, gates, gates)


def _layer_norm_rows(z, g, b):
    mu = jnp.mean(z, axis=1, keepdims=True)
    zc = z - mu
    var = jnp.mean(zc * zc, axis=1, keepdims=True)
    return zc * lax.rsqrt(var + LN_EPS) * g + b


def _matmul_ln_kernel(h_ref, w_ref, x_ref, g_ref, b_ref, xo_ref, xb_ref, wbf_ref, *, alpha):
    i = pl.program_id(0)

    def emit(ws):
        mix = jnp.dot(h_ref[...], ws[0], preferred_element_type=F32)
        y = _layer_norm_rows(alpha * x_ref[...] + mix, g_ref[...], b_ref[...])
        xo_ref[...] = y
        xb_ref[...] = y.astype(BF16)

    _with_bf16_weights(i == 0, i != 0, [w_ref], [wbf_ref], emit)


def _matmul_ln(h, w, x, g, b, layer, alpha, tm):
    s, k = h.shape
    n = w.shape[-1]
    row_spec = pl.BlockSpec((tm, n), lambda i: (i, 0))
    par_spec = pl.BlockSpec((None, 1, n), lambda i: (layer, 0, 0))
    return pl.pallas_call(
        functools.partial(_matmul_ln_kernel, alpha=alpha),
        grid=(s // tm,),
        in_specs=[pl.BlockSpec((tm, k), lambda i: (i, 0)),
                  pl.BlockSpec((None, k, n), lambda i: (layer, 0, 0), pipeline_mode=pl.Buffered(1)),
                  row_spec, par_spec, par_spec],
        out_specs=[row_spec, row_spec],
        out_shape=[jax.ShapeDtypeStruct((s, n), F32), jax.ShapeDtypeStruct((s, n), BF16)],
        scratch_shapes=[pltpu.VMEM((k, n), BF16)],
        compiler_params=_cparams(("arbitrary",), 56),
        name="outproj_ln",
    )(h, w, x, g, b)


def _add_ln_kernel(x_ref, f_ref, g_ref, b_ref, xo_ref, xb_ref, *, alpha):
    y = _layer_norm_rows(alpha * x_ref[...] + f_ref[...], g_ref[...], b_ref[...])
    xo_ref[...] = y
    xb_ref[...] = y.astype(BF16)


def _add_ln(x, f, g, b, layer, alpha, tm):
    s, d = x.shape
    row_spec = pl.BlockSpec((tm, d), lambda i: (i, 0))
    par_spec = pl.BlockSpec((None, 1, d), lambda i: (layer, 0, 0))
    return pl.pallas_call(
        functools.partial(_add_ln_kernel, alpha=alpha),
        grid=(s // tm,),
        in_specs=[row_spec, row_spec, par_spec, par_spec],
        out_specs=[row_spec, row_spec],
        out_shape=[jax.ShapeDtypeStruct((s, d), F32), jax.ShapeDtypeStruct((s, d), BF16)],
        compiler_params=_cparams(("arbitrary",), 32),
        name="add_ln",
    )(x, f, g, b)


def _grouped_weights(tf_ref, tg_ref, tr_ref, ge_ref, meta_ref, w_hbms, wbuf, wbf_refs, sem, emit, *,
                     layer, tn, half_rows):
    j = pl.program_id(0)
    t = pl.program_id(1)
    n_used, n_groups = meta_ref[0], meta_ref[1]
    active = t < n_used
    first = tf_ref[t] == 1
    g = tg_ref[t]
    block = j * n_groups + g
    slot = block % 2

    def copies(expert, col_block, sl):
        col = pl.multiple_of(col_block * tn, tn)
        return [pltpu.make_async_copy(w.at[layer, expert, :, pl.ds(col, tn)], wbuf.at[sl, m], sem.at[sl])
                for m, w in enumerate(w_hbms)]

    @pl.when(active & first)
    def _stream():
        @pl.when(block == 0)
        def _cold_start():
            for cp in copies(ge_ref[0], 0, 0):
                cp.start()

        for cp in copies(ge_ref[g], j, slot):
            cp.wait()
        wrap = g + 1 == n_groups
        g_next = jnp.where(wrap, 0, g + 1)
        j_next = jnp.where(wrap, j + 1, j)

        @pl.when(j_next < pl.num_programs(0))
        def _prefetch():
            for cp in copies(ge_ref[g_next], j_next, 1 - slot):
                cp.start()

    half_tile = tr_ref[t] <= half_rows
    for half in (False, True):
        sel = half_tile if half else jnp.logical_not(half_tile)

        @pl.when(active & first & sel)
        def _fresh(half=half):
            ws = [wbuf[slot, m].astype(BF16) for m in range(len(w_hbms))]
            for dst, w in zip(wbf_refs, ws):
                dst[...] = w
            emit(ws, half)

        @pl.when(active & jnp.logical_not(first) & sel)
        def _cached(half=half):
            emit([r[...] for r in wbf_refs], half)

    return active


def _gate_up_kernel(tf_ref, tg_ref, tr_ref, ge_ref, meta_ref, x_ref, wg_hbm, wu_hbm, o_ref,
                    wbuf, wgb_ref, wub_ref, sem, *, layer, tn):
    half_rows = x_ref.shape[0] // 2

    def emit(ws, half):
        rows = half_rows if half else x_ref.shape[0]
        x = x_ref[0:rows, :]
        g = jnp.dot(x, ws[0], preferred_element_type=F32)
        u = jnp.dot(x, ws[1], preferred_element_type=F32)
        o_ref[0:rows, :] = (g * jax.nn.sigmoid(g) * u).astype(BF16)
        if half:
            o_ref[rows:, :] = jnp.zeros((x_ref.shape[0] - rows, o_ref.shape[1]), BF16)

    active = _grouped_weights(tf_ref, tg_ref, tr_ref, ge_ref, meta_ref, [wg_hbm, wu_hbm], wbuf, [wgb_ref, wub_ref],
                              sem, emit, layer=layer, tn=tn, half_rows=half_rows)

    @pl.when(jnp.logical_not(active))
    def _unused_tile():
        o_ref[...] = jnp.zeros_like(o_ref)


def _row_tile(j, t, tf, tg, tr, ge, meta):
    return (jnp.minimum(t, meta[0] - 1), 0)


def _out_tile(j, t, tf, tg, tr, ge, meta):
    return (t, j)


def _gate_up(groups, x_sorted, w_gate, w_up, layer, tm, tn):
    n_slots, k = x_sorted.shape
    n = w_gate.shape[-1]
    return pl.pallas_call(
        functools.partial(_gate_up_kernel, layer=layer, tn=tn),
        grid_spec=pltpu.PrefetchScalarGridSpec(
            num_scalar_prefetch=5,
            grid=(n // tn, n_slots // tm),
            in_specs=[pl.BlockSpec((tm, k), _row_tile), pl.BlockSpec(memory_space=pl.ANY),
                      pl.BlockSpec(memory_space=pl.ANY)],
            out_specs=pl.BlockSpec((tm, tn), _out_tile),
            scratch_shapes=[pltpu.VMEM((2, 2, k, tn), F32), pltpu.VMEM((k, tn), BF16), pltpu.VMEM((k, tn), BF16),
                            pltpu.SemaphoreType.DMA((2,))],
        ),
        out_shape=jax.ShapeDtypeStruct((n_slots, n), BF16),
        compiler_params=_cparams(("arbitrary", "arbitrary"), 52),
        name="ffn_gate_up",
    )(*groups, x_sorted, w_gate, w_up)


def _down_kernel(tf_ref, tg_ref, tr_ref, ge_ref, meta_ref, h_ref, w_hbm, o_ref, wbuf, wbf_ref, sem, *,
                 layer, tn):
    half_rows = h_ref.shape[0] // 2

    def emit(ws, half):
        rows = half_rows if half else h_ref.shape[0]
        o_ref[0:rows, :] = jnp.dot(h_ref[0:rows, :], ws[0], preferred_element_type=F32)
        if half:
            o_ref[rows:, :] = jnp.zeros((h_ref.shape[0] - rows, o_ref.shape[1]), F32)

    active = _grouped_weights(tf_ref, tg_ref, tr_ref, ge_ref, meta_ref, [w_hbm], wbuf, [wbf_ref], sem, emit,
                              layer=layer, tn=tn, half_rows=half_rows)

    @pl.when(jnp.logical_not(active))
    def _unused_tile():
        o_ref[...] = jnp.zeros_like(o_ref)


def _down(groups, h_sorted, w_down, layer, tm, tn):
    n_slots, k = h_sorted.shape
    n = w_down.shape[-1]
    return pl.pallas_call(
        functools.partial(_down_kernel, layer=layer, tn=tn),
        grid_spec=pltpu.PrefetchScalarGridSpec(
            num_scalar_prefetch=5,
            grid=(n // tn, n_slots // tm),
            in_specs=[pl.BlockSpec((tm, k), _row_tile), pl.BlockSpec(memory_space=pl.ANY)],
            out_specs=pl.BlockSpec((tm, tn), _out_tile),
            scratch_shapes=[pltpu.VMEM((2, 1, k, tn), F32), pltpu.VMEM((k, tn), BF16),
                            pltpu.SemaphoreType.DMA((2,))],
        ),
        out_shape=jax.ShapeDtypeStruct((n_slots, n), F32),
        compiler_params=_cparams(("arbitrary", "arbitrary"), 56),
        name="ffn_down",
    )(*groups, h_sorted, w_down)


def _single_group(n_tiles, tm):
    return (jnp.zeros((n_tiles,), jnp.int32).at[0].set(1),
            jnp.zeros((n_tiles,), jnp.int32),
            jnp.full((n_tiles,), tm, jnp.int32),
            jnp.zeros((N_EXPERTS,), jnp.int32),
            jnp.array([n_tiles, 1], jnp.int32))


def _router_kernel(x_ref, rw_ref, rb_ref, o_ref, cnt_ref, carry_ref, *, tm):
    i = pl.program_id(0)

    @pl.when(i == 0)
    def _init():
        carry_ref[...] = jnp.zeros_like(carry_ref)

    logits = jnp.dot(x_ref[...], rw_ref[...], preferred_element_type=F32,
                     precision=lax.Precision.HIGHEST) + rb_ref[...]
    lane = lax.broadcasted_iota(jnp.int32, (tm, LANES), 1).astype(F32)
    lg = jnp.where(lane < N_EXPERTS, logits, -jnp.inf)
    v1 = jnp.max(lg, axis=1, keepdims=True)
    e1 = jnp.min(jnp.where(lg == v1, lane, float(LANES)), axis=1, keepdims=True)
    lg2 = jnp.where(lane == e1, -jnp.inf, lg)
    v2 = jnp.max(lg2, axis=1, keepdims=True)
    e2 = jnp.min(jnp.where(lg2 == v2, lane, float(LANES)), axis=1, keepdims=True)
    t2 = jnp.exp(v2 - v1)
    w1 = 1.0 / (1.0 + t2)
    w2 = t2 / (1.0 + t2)
    pick1 = lane == e1
    pick2 = lane == e2
    onehot = jnp.where(pick1 | pick2, 1.0, 0.0)
    row = lax.broadcasted_iota(jnp.int32, (tm, tm), 0)
    col = lax.broadcasted_iota(jnp.int32, (tm, tm), 1)
    before = (col < row).astype(BF16)
    seen = jnp.dot(before, onehot.astype(BF16), preferred_element_type=F32) + carry_ref[...]
    r1 = jnp.sum(jnp.where(pick1, seen, 0.0), axis=1, keepdims=True)
    r2 = jnp.sum(jnp.where(pick2, seen, 0.0), axis=1, keepdims=True)
    carry_ref[...] = carry_ref[...] + jnp.sum(onehot, axis=0, keepdims=True)
    o_ref[...] = jnp.where(lane == 0, e1, jnp.where(lane == 1, e2, jnp.where(lane == 2, w1,
                 jnp.where(lane == 3, w2, jnp.where(lane == 4, r1, jnp.where(lane == 5, r2, 0.0))))))
    cnt_ref[...] = carry_ref[...]


def _router(x, rw_pad, rb_pad, tm):
    s, d = x.shape
    return pl.pallas_call(
        functools.partial(_router_kernel, tm=tm),
        grid=(s // tm,),
        in_specs=[pl.BlockSpec((tm, d), lambda i: (i, 0)),
                  pl.BlockSpec((d, LANES), lambda i: (0, 0)),
                  pl.BlockSpec((1, LANES), lambda i: (0, 0))],
        out_specs=[pl.BlockSpec((tm, LANES), lambda i: (i, 0)), pl.BlockSpec((1, LANES), lambda i: (0, 0))],
        out_shape=[jax.ShapeDtypeStruct((s, LANES), F32), jax.ShapeDtypeStruct((1, LANES), F32)],
        scratch_shapes=[pltpu.VMEM((1, LANES), F32)],
        compiler_params=_cparams(("arbitrary",), 32),
        name="router",
    )(x, rw_pad, rb_pad)


def _scatter_copies(x_ref, o_hbm, p1_ref, p2_ref, base, r, sem):
    return (pltpu.make_async_copy(x_ref.at[r], o_hbm.at[p1_ref[base + r]], sem),
            pltpu.make_async_copy(x_ref.at[r], o_hbm.at[p2_ref[base + r]], sem))


def _dispatch_kernel(p1_ref, p2_ref, x_ref, init_hbm, o_hbm, sem, *, tt):
    del init_hbm
    base = pl.program_id(0) * tt

    def issue(r, carry):
        for cp in _scatter_copies(x_ref, o_hbm, p1_ref, p2_ref, base, r, sem):
            cp.start()
        return carry

    lax.fori_loop(0, tt, issue, 0, unroll=8)

    def drain(r, carry):
        for cp in _scatter_copies(x_ref, o_hbm, p1_ref, p2_ref, base, r, sem):
            cp.wait()
        return carry

    lax.fori_loop(0, tt, drain, 0, unroll=8)


def _dispatch(pos1, pos2, x_bf, n_slots, tt):
    s, d = x_bf.shape
    x_slabs = x_bf.reshape(s, d // LANES, LANES)
    out = pl.pallas_call(
        functools.partial(_dispatch_kernel, tt=tt),
        grid_spec=pltpu.PrefetchScalarGridSpec(
            num_scalar_prefetch=2,
            grid=(s // tt,),
            in_specs=[pl.BlockSpec((tt, d // LANES, LANES), lambda i, p1, p2: (i, 0, 0)),
                      pl.BlockSpec(memory_space=pl.ANY)],
            out_specs=pl.BlockSpec(memory_space=pl.ANY),
            scratch_shapes=[pltpu.SemaphoreType.DMA],
        ),
        out_shape=jax.ShapeDtypeStruct((n_slots, d // LANES, LANES), BF16),
        input_output_aliases={3: 0},
        compiler_params=_cparams(("arbitrary",), 32),
        name="moe_dispatch",
    )(pos1, pos2, x_slabs, jnp.zeros((n_slots, d // LANES, LANES), BF16))
    return out.reshape(n_slots, d)


def _row_copy(src_hbm, src_row, buf, dst_row, sem):
    return pltpu.make_async_copy(src_hbm.at[pl.ds(src_row, 1), :], buf.at[pl.ds(dst_row, 1), :], sem)


def _combine_ln_kernel(p1_ref, p2_ref, y_hbm, x_ref, route_ref, g_ref, b_ref, xo_ref, xb_ref,
                       buf, sem, *, tc, alpha):
    i = pl.program_id(0)

    def copies(step, slot, r):
        t = step * tc + r
        return (_row_copy(y_hbm, p1_ref[t], buf.at[slot, 0], r, sem.at[slot]),
                _row_copy(y_hbm, p2_ref[t], buf.at[slot, 1], r, sem.at[slot]))

    def issue(step, slot):
        def body(r, carry):
            for cp in copies(step, slot, r):
                cp.start()
            return carry

        lax.fori_loop(0, tc, body, 0, unroll=8)

    @pl.when(i == 0)
    def _first():
        issue(0, 0)

    @pl.when(i + 1 < pl.num_programs(0))
    def _prefetch():
        issue(i + 1, (i + 1) % 2)

    slot = i % 2

    def drain(r, carry):
        for cp in copies(i, slot, r):
            cp.wait()
        return carry

    lax.fori_loop(0, tc, drain, 0, unroll=8)
    route = route_ref[...]
    f = route[:, 2:3] * buf[slot, 0] + route[:, 3:4] * buf[slot, 1]
    y = _layer_norm_rows(alpha * x_ref[...] + f, g_ref[...], b_ref[...])
    xo_ref[...] = y
    xb_ref[...] = y.astype(BF16)


def _combine_ln(pos1, pos2, y_sorted, x, route, g, b, layer, alpha, tc):
    s, d = x.shape
    row_spec = pl.BlockSpec((tc, d), lambda i, p1, p2: (i, 0))
    par_spec = pl.BlockSpec((None, 1, d), lambda i, p1, p2: (layer, 0, 0))
    return pl.pallas_call(
        functools.partial(_combine_ln_kernel, tc=tc, alpha=alpha),
        grid_spec=pltpu.PrefetchScalarGridSpec(
            num_scalar_prefetch=2,
            grid=(s // tc,),
            in_specs=[pl.BlockSpec(memory_space=pl.ANY), row_spec,
                      pl.BlockSpec((tc, LANES), lambda i, p1, p2: (i, 0)), par_spec, par_spec],
            out_specs=[row_spec, row_spec],
            scratch_shapes=[pltpu.VMEM((2, 2, tc, d), F32), pltpu.SemaphoreType.DMA((2,))],
        ),
        out_shape=[jax.ShapeDtypeStruct((s, d), F32), jax.ShapeDtypeStruct((s, d), BF16)],
        compiler_params=_cparams(("arbitrary",), 32),
        name="moe_combine_ln",
    )(pos1, pos2, y_sorted, x, route, g, b)


def _moe_plan(route, counts, tm, n_tiles):
    expert = route[:, 0:2].astype(jnp.int32)
    rank = route[:, 4:6].astype(jnp.int32)
    count = counts[0, :N_EXPERTS].astype(jnp.int32)
    padded = ((count + tm - 1) // tm) * tm
    ends = jnp.cumsum(padded)
    starts = ends - padded
    pos = starts[expert] + rank
    n_used = ends[-1] // tm
    tile = jnp.arange(n_tiles, dtype=jnp.int32)
    last = jnp.minimum(tile, n_used - 1)
    tile_expert = jnp.sum(last[:, None] * tm >= ends[None, :], axis=1).astype(jnp.int32)
    prev = jnp.concatenate([jnp.full((1,), -1, jnp.int32), tile_expert[:-1]])
    tile_first = (tile_expert != prev).astype(jnp.int32)
    nonempty = count > 0
    group_of_expert = jnp.cumsum(nonempty.astype(jnp.int32)) - 1
    experts = jnp.arange(N_EXPERTS, dtype=jnp.int32)
    group_expert = jnp.argsort(jnp.where(nonempty, experts, experts + N_EXPERTS)).astype(jnp.int32)
    meta = jnp.stack([n_used, jnp.sum(nonempty)]).astype(jnp.int32)
    tile_rows = jnp.clip(count[tile_expert] - (last * tm - starts[tile_expert]), 0, tm).astype(jnp.int32)
    return pos, (tile_first, group_of_expert[tile_expert], tile_rows, group_expert, meta)


def _rope_tables(seq):
    pos = jnp.arange(seq, dtype=F32)
    inv = ROPE_THETA ** (-jnp.arange(0, QK_DIM, 2, dtype=F32) / QK_DIM)
    ang = pos[:, None] * inv[None, :]
    cos, sin = jnp.cos(ang), jnp.sin(ang)
    return jnp.tile(cos, (1, 4)), jnp.tile(jnp.concatenate([-sin, sin], axis=1), (1, 2))


def _pad_lanes(a):
    return jnp.pad(a, ((0, 0), (0, LANES - a.shape[1])))


def kernel(x, w_in, b_forget, w_pool_group, pool_scale, diff_lambda, diff_norm_gain, w_branch_a, w_branch_b, w_branch_c, b_gate, w_out, ln1_g, ln1_b, ln2_g, ln2_b, ffn_w_gate, ffn_w_up, ffn_w_down, router_w, router_b, expert_w_gate, expert_w_up, expert_w_down):
    batch, s, d = x.shape
    assert batch == 1 and d == D_MODEL and w_in.shape[-1] == N_MAIN + N_HEADS
    depth = w_in.shape[0]
    alpha = (2 * depth) ** 0.25

    tm = min(1024, s)
    tq = min(1024, s)
    tk = tq // 4
    te = min(512, s)
    tg = min(256, s)
    n_ff = 512
    n_tiles = 2 * s // te + N_EXPERTS
    n_slots = n_tiles * te

    cos_t, sin_t = _rope_tables(s)
    w_in_t = jnp.swapaxes(w_in, 1, 2)
    b_gate_flat = b_gate.reshape(depth, 1, 3 * D_MODEL)
    pool_scale3 = pool_scale.reshape(depth, 1, SEG)
    gain3 = diff_norm_gain.reshape(depth, 1, HEAD_DIM)
    ln = [p.reshape(depth, 1, D_MODEL) for p in (ln1_g, ln1_b, ln2_g, ln2_b)]
    dense_w = [w.reshape(w.shape[0], 1, *w.shape[1:]) for w in (ffn_w_gate, ffn_w_up, ffn_w_down)]

    xf = x.reshape(s, d)
    xb = xf.astype(BF16)
    for layer in range(depth):
        lambda_init = 0.8 - 0.6 * math.exp(-0.3 * layer)
        plain, rope, gates = _inproj_all(xb, w_in_t, b_gate_flat, cos_t, sin_t, layer, tm)
        b_fg = _pad_lanes(b_forget[layer].reshape(1, N_HEADS))
        q_aug_t, k_aug = _forget_prep(xb, w_in_t, b_fg, plain, layer, te, tq)
        segs_t = _transpose_segs(plain, rope, te, tk)
        y_a = _fox_attn(q_aug_t, k_aug, segs_t, tq, tk)
        y_b = _pool(plain, w_pool_group, pool_scale3, layer, tm)
        y_c = _diff_attn(rope, segs_t, diff_lambda, gain3, layer, lambda_init, tq, tk)
        h = _merge(y_a, y_b, y_c, w_branch_a, w_branch_b, w_branch_c, gates, layer, tm, 512)
        xf, xb = _matmul_ln(h, w_out, xf, ln[0], ln[1], layer, alpha, tg)

        j = layer // 2
        if layer % 2 == 0:
            hidden = _gate_up(_single_group(s // tm, tm), xb, dense_w[0], dense_w[1], j, tm, n_ff)
            f = _down(_single_group(s // te, te), hidden, dense_w[2], j, te, 512)
            xf, xb = _add_ln(xf, f, ln[2], ln[3], layer, alpha, tg)
        else:
            route, counts = _router(xf, _pad_lanes(router_w[j]), _pad_lanes(router_b[j].reshape(1, N_EXPERTS)), te)
            pos, groups = _moe_plan(route, counts, te, n_tiles)
            x_sorted = _dispatch(pos[:, 0], pos[:, 1], xb, n_slots, te)
            hidden = _gate_up(groups, x_sorted, expert_w_gate, expert_w_up, j, te, n_ff)
            y_sorted = _down(groups, hidden, expert_w_down, j, te, 512)
            xf, xb = _combine_ln(pos[:, 0], pos[:, 1], y_sorted, xf, route, ln[2], ln[3], layer, alpha, tg)
    return xf.reshape(batch, s, d)
```

```python
import functools
import math

import jax
import jax.numpy as jnp
from jax import lax
from jax.experimental import pallas as pl
from jax.experimental.pallas import tpu as pltpu

BF16 = jnp.bfloat16
F32 = jnp.float32

D_MODEL = 2048
N_HEADS = 8
HEAD_DIM = 128
QK_DIM = 64
SEG = 1024
N_MAIN = 13 * SEG
POOL_WINDOWS = (2, 4, 8, 16)
POOL_GROUP_DIM = 256
POOL_HALO = 16
N_EXPERTS = 8
ROPE_THETA = 10000.0
LN_EPS = 1e-5
LOG2E = math.log2(math.e)
FOX_Q_SCALE = HEAD_DIM ** -0.5 * LOG2E
DIFF_Q_SCALE = QK_DIM ** -0.5 * LOG2E
LANES = 128

MIB = 2 ** 20


def _cparams(semantics, vmem_mib):
    return pltpu.CompilerParams(dimension_semantics=semantics, vmem_limit_bytes=vmem_mib * MIB)


def _with_bf16_weights(fresh, cached, w_refs, wbf_refs, emit, transpose=False):
    @pl.when(fresh)
    def _fresh():
        ws = [(w[...].T if transpose else w[...]).astype(BF16) for w in w_refs]
        for dst, w in zip(wbf_refs, ws):
            dst[...] = w
        emit(ws)

    @pl.when(cached)
    def _cached():
        emit([r[...] for r in wbf_refs])


def _rope_store(acc, cos, sin, scale, o_ref):
    lane = lax.broadcasted_iota(jnp.int32, (1, LANES), 1)
    first_half = (lane % QK_DIM) < (QK_DIM // 2)
    for c in range(SEG // LANES):
        a = acc[:, c * LANES:(c + 1) * LANES]
        partner = jnp.where(first_half, pltpu.roll(a, LANES - QK_DIM // 2, 1), pltpu.roll(a, QK_DIM // 2, 1))
        o_ref[:, c * LANES:(c + 1) * LANES] = ((a * cos + partner * sin) * scale).astype(BF16)


def _inproj_kernel(x_ref, w_ref, *rest, mode, first_scale):
    o_ref, wbf_ref = rest[-2:]
    j = pl.program_id(0)
    i = pl.program_id(1)

    def emit(ws):
        acc = jnp.dot(x_ref[...], ws[0], preferred_element_type=F32)
        scale = jnp.where(j == 0, first_scale, 1.0).astype(F32)
        if mode == "plain":
            o_ref[...] = (acc * scale).astype(BF16)
        elif mode == "rope":
            _rope_store(acc, rest[0][...], rest[1][...], scale, o_ref)
        else:
            o_ref[...] = jax.nn.sigmoid(acc + rest[0][...]).astype(BF16)

    _with_bf16_weights(i == 0, i != 0, [w_ref], [wbf_ref], emit, transpose=True)


PLAIN_SEGS = (0, 1, 2, 5, 6)
ROPE_SEGS = (3, 4)
GATE_SEGS = (7, 8, 9, 10, 11, 12)


def _inproj(x_bf, w_in_t, extras, extra_specs, layer, tm, mode, segs, first_scale):
    s, d = x_bf.shape
    jump = next((k for k in range(1, len(segs)) if segs[k] != segs[k - 1] + 1), len(segs))
    gap = segs[jump] - segs[jump - 1] - 1 if jump < len(segs) else 0
    seg_of = lambda j: segs[0] + j + jnp.where(j >= jump, gap, 0)
    return pl.pallas_call(
        functools.partial(_inproj_kernel, mode=mode, first_scale=first_scale),
        grid=(len(segs), s // tm),
        in_specs=[pl.BlockSpec((tm, d), lambda j, i: (i, 0)),
                  pl.BlockSpec((None, SEG, d), lambda j, i: (layer, seg_of(j), 0))] + extra_specs,
        out_specs=pl.BlockSpec((tm, SEG), lambda j, i: (i, j)),
        out_shape=jax.ShapeDtypeStruct((s, len(segs) * SEG), BF16),
        scratch_shapes=[pltpu.VMEM((d, SEG), BF16)],
        compiler_params=_cparams(("arbitrary", "arbitrary"), 48),
        name="inproj_" + mode,
    )(x_bf, w_in_t, *extras)


def _inproj_all(x_bf, w_in_t, b_gate_flat, cos_t, sin_t, layer, tm):
    table_spec = pl.BlockSpec((tm, LANES), lambda j, i: (i, 0))
    plain = _inproj(x_bf, w_in_t, [], [], layer, tm, "plain", PLAIN_SEGS, FOX_Q_SCALE)
    rope = _inproj(x_bf, w_in_t, [cos_t, sin_t], [table_spec, table_spec], layer, tm, "rope", ROPE_SEGS,
                   DIFF_Q_SCALE)
    gates = _inproj(x_bf, w_in_t, [b_gate_flat], [pl.BlockSpec((None, 1, SEG), lambda j, i: (layer, 0, j))],
                    layer, tm, "gate", GATE_SEGS, 1.0)
    return plain, rope, gates


def _split3(v):
    hi = v.astype(BF16)
    r = v - hi.astype(F32)
    mid = r.astype(BF16)
    lo = (r - mid.astype(F32)).astype(BF16)
    return hi, mid, lo


def _forget_prep_kernel(x_ref, wfg_ref, bf_ref, fq_ref, fk_ref, qa_ref, ka_ref, carry_ref, *, tm):
    i = pl.program_id(0)

    @pl.when(i == 0)
    def _init():
        carry_ref[...] = jnp.zeros_like(carry_ref)

    w_fg = jnp.concatenate([wfg_ref[...], jnp.zeros((LANES - N_HEADS, wfg_ref.shape[1]), F32)], axis=0)
    z = lax.dot_general(x_ref[...], w_fg.astype(BF16), (((1,), (1,)), ((), ())),
                        preferred_element_type=F32) + bf_ref[...]
    log_f = -(jnp.maximum(-z, 0.0) + jnp.log1p(jnp.exp(-jnp.abs(z)))) * LOG2E
    row = lax.broadcasted_iota(jnp.int32, (tm, tm), 0)
    col = lax.broadcasted_iota(jnp.int32, (tm, tm), 1)
    tri = (col <= row).astype(BF16)
    hi, mid, lo = _split3(log_f)
    cum = (jnp.dot(tri, hi, preferred_element_type=F32) + jnp.dot(tri, mid, preferred_element_type=F32)
           + jnp.dot(tri, lo, preferred_element_type=F32)) + carry_ref[...]
    carry_ref[...] = cum[tm - 1:tm, :]
    c_hi, c_mid, c_lo = (p.astype(F32) for p in _split3(cum))
    lane = lax.broadcasted_iota(jnp.int32, (1, LANES), 1)
    for h in range(N_HEADS):
        ch, cm, cl = c_hi[:, h:h + 1], c_mid[:, h:h + 1], c_lo[:, h:h + 1]
        q_extra = jnp.where(lane == 0, ch, jnp.where(lane == 1, cm, jnp.where(lane == 2, cl,
                  jnp.where(lane < 6, 1.0, 0.0))))
        k_extra = jnp.where(lane < 3, 1.0, jnp.where(lane == 3, -ch, jnp.where(lane == 4, -cm,
                  jnp.where(lane == 5, -cl, 0.0))))
        q_aug = jnp.concatenate([fq_ref[:, h * HEAD_DIM:(h + 1) * HEAD_DIM].astype(F32), q_extra], axis=1)
        qa_ref[h] = q_aug.T.astype(BF16)
        ka_ref[h, :, 0:HEAD_DIM] = fk_ref[:, h * HEAD_DIM:(h + 1) * HEAD_DIM]
        ka_ref[h, :, HEAD_DIM:2 * HEAD_DIM] = k_extra.astype(BF16)


def _forget_prep(x_bf, w_in_t, b_fg, proj, layer, tm, tq):
    s, d = x_bf.shape
    per_tile = tq // tm
    return pl.pallas_call(
        functools.partial(_forget_prep_kernel, tm=tm),
        grid=(s // tm,),
        in_specs=[
            pl.BlockSpec((tm, d), lambda i: (i, 0)),
            pl.BlockSpec((None, N_HEADS, d), lambda i: (layer, N_MAIN // N_HEADS, 0)),
            pl.BlockSpec((1, LANES), lambda i: (0, 0)),
            pl.BlockSpec((tm, SEG), lambda i: (i, 0)),
            pl.BlockSpec((tm, SEG), lambda i: (i, 1)),
        ],
        out_specs=[pl.BlockSpec((N_HEADS, None, 2 * HEAD_DIM, tm),
                                lambda i: (0, i // per_tile, 0, i % per_tile)),
                   pl.BlockSpec((N_HEADS, tm, 2 * HEAD_DIM), lambda i: (0, i, 0))],
        out_shape=[jax.ShapeDtypeStruct((N_HEADS, s // tq, 2 * HEAD_DIM, tq), BF16),
                   jax.ShapeDtypeStruct((N_HEADS, s, 2 * HEAD_DIM), BF16)],
        scratch_shapes=[pltpu.VMEM((1, LANES), F32)],
        compiler_params=_cparams(("arbitrary",), 32),
        name="forget_prep",
    )(x_bf, w_in_t, b_fg, proj, proj)


def _transpose_kernel(fox_v_ref, diff_q_ref, diff_v_ref, o_ref, *, tk):
    for g, ref in enumerate((fox_v_ref, diff_q_ref, diff_v_ref)):
        xt = ref[...].astype(F32).T
        for c in range(o_ref.shape[1]):
            o_ref[g, c] = xt[:, c * tk:(c + 1) * tk].astype(BF16)


def _transpose_segs(plain, rope, tm, tk):
    s = plain.shape[0]
    seg_spec = lambda blk: pl.BlockSpec((tm, SEG), lambda i: (i, blk))
    return pl.pallas_call(
        functools.partial(_transpose_kernel, tk=tk),
        grid=(s // tm,),
        in_specs=[seg_spec(PLAIN_SEGS.index(2)), seg_spec(ROPE_SEGS.index(3)), seg_spec(PLAIN_SEGS.index(5))],
        out_specs=pl.BlockSpec((3, tm // tk, SEG, tk), lambda i: (0, i, 0, 0)),
        out_shape=jax.ShapeDtypeStruct((3, s // tk, SEG, tk), BF16),
        compiler_params=_cparams(("arbitrary",), 40),
        name="transpose_segs",
    )(plain, rope, plain)


ONES_ROWS = 16


class _AttnState:
    S_SLOTS = 4
    P_SLOTS = 2
    N_REFS = 2 * S_SLOTS + 2 * P_SLOTS + 2

    def __init__(self, refs):
        a, b = self.S_SLOTS, self.S_SLOTS + self.P_SLOTS
        self.s = refs[0:a]
        self.p = refs[a:b]
        self.alpha = refs[b:b + self.P_SLOTS]
        c = b + self.P_SLOTS
        self.chunk_max = refs[c:c + self.S_SLOTS]
        self.m, self.acc = refs[c + self.S_SLOTS:]

    @classmethod
    def scratch(cls, tq, tk):
        row = pltpu.VMEM((1, tq), F32)
        return ([pltpu.VMEM((tk, tq), F32)] * cls.S_SLOTS + [pltpu.VMEM((tk, tq), BF16)] * cls.P_SLOTS
                + [row] * (cls.P_SLOTS + cls.S_SLOTS) + [row, pltpu.VMEM((HEAD_DIM + ONES_ROWS, tq), F32)])

    def init(self):
        self.m[...] = jnp.full_like(self.m, -jnp.inf)
        self.acc[...] = jnp.zeros_like(self.acc)
        self.p[1][...] = jnp.zeros_like(self.p[1])
        self.alpha[1][...] = jnp.ones_like(self.alpha[1])

    def normalized(self):
        return self.acc[0:HEAD_DIM, :] / self.acc[HEAD_DIM:HEAD_DIM + 1, :]


def _attn_pipeline(i, q_t, k_ref, v_t_ref, states, tq, tk):
    n_sub = tq // tk
    assert tq == n_sub * tk and n_sub == _AttnState.S_SLOTS

    def scores(c, slot, q0=0):
        k = k_ref[pl.ds(pl.multiple_of(c * tk, tk), tk), :]
        for st, q in zip(states, q_t):
            s = jnp.dot(k, q[:, q0:], preferred_element_type=F32)
            st.s[slot][:, q0:] = s
            st.chunk_max[slot][:, q0:] = jnp.max(s, axis=0, keepdims=True)

    def softmax(s_slot, slot, k0=None, q0=0):
        for st in states:
            s = st.s[s_slot][:, q0:]
            if k0 is None:
                chunk_max = st.chunk_max[s_slot][:, q0:]
            else:
                key = lax.broadcasted_iota(jnp.int32, (tk, tk), 0)
                query = lax.broadcasted_iota(jnp.int32, (tk, tk), 1)
                head = jnp.where(key <= query, s[:, :tk], -jnp.inf)
                s = head if s.shape[1] == tk else jnp.concatenate([head, s[:, tk:]], axis=1)
                chunk_max = jnp.max(s, axis=0, keepdims=True)
            m_prev = st.m[:, q0:]
            m_new = jnp.maximum(m_prev, chunk_max)
            st.m[:, q0:] = m_new
            st.alpha[slot][:, q0:] = jnp.exp2(m_prev - m_new)
            st.p[slot][:, q0:] = jnp.exp2(s - m_new).astype(BF16)

    def values(c, slot, q0=0):
        v_t = jnp.concatenate([v_t_ref[c], jnp.ones((ONES_ROWS, tk), BF16)], axis=0)
        for st in states:
            st.acc[:, q0:] = st.alpha[slot][:, q0:] * st.acc[:, q0:] + jnp.dot(
                v_t, st.p[slot][:, q0:], preferred_element_type=F32)

    for st in states:
        st.init()
    scores(0, 0)
    scores(1, 1)

    def body(block, carry):
        c = n_sub * block
        for u in range(n_sub):
            scores(c + u + 2, (u + 2) % n_sub)
            softmax(u, u % 2)
            values(jnp.maximum(c + u - 1, 0), (u + 1) % 2)
        return carry

    lax.fori_loop(0, i, body, 0)
    c = n_sub * i
    for u in range(n_sub):
        if u + 2 < n_sub:
            scores(c + u + 2, u + 2, q0=(u + 2) * tk)
        softmax(u, u % 2, k0=(c + u) * tk, q0=u * tk)
        values(jnp.maximum(c + u - 1, 0), (u + 1) % 2, q0=max(u - 1, 0) * tk)
    values(c + n_sub - 1, (n_sub - 1) % 2, q0=(n_sub - 1) * tk)


def _fox_attn_kernel(q_ref, k_ref, v_ref, o_ref, *scratch, tq, tk):
    st = _AttnState(scratch)
    _attn_pipeline(pl.program_id(1), [q_ref[...]], k_ref, v_ref, [st], tq, tk)
    o_ref[...] = st.normalized().T.astype(BF16)


def _fox_attn(q_aug_t, k_aug, segs_t, tq, tk):
    _, s, _ = k_aug.shape
    return pl.pallas_call(
        functools.partial(_fox_attn_kernel, tq=tq, tk=tk),
        grid=(N_HEADS, s // tq),
        in_specs=[
            pl.BlockSpec((None, None, 2 * HEAD_DIM, tq), lambda h, i: (h, i, 0, 0)),
            pl.BlockSpec((None, s, 2 * HEAD_DIM), lambda h, i: (h, 0, 0)),
            pl.BlockSpec((None, s // tk, HEAD_DIM, tk), lambda h, i: (0, 0, h, 0)),
        ],
        out_specs=pl.BlockSpec((tq, HEAD_DIM), lambda h, i: (i, h)),
        out_shape=jax.ShapeDtypeStruct((s, SEG), BF16),
        scratch_shapes=_AttnState.scratch(tq, tk),
        compiler_params=_cparams(("arbitrary", "arbitrary"), 40),
        name="fox_attn",
    )(q_aug_t, k_aug, segs_t)


def _diff_attn_kernel(q_ref, k_ref, v_ref, dl_ref, gain_ref, o_ref, *scratch, tq, tk, lambda_init):
    st1 = _AttnState(scratch[:_AttnState.N_REFS])
    st2 = _AttnState(scratch[_AttnState.N_REFS:])
    q_t = jnp.concatenate([q_ref[c] for c in range(tq // tk)], axis=1)
    channel = lax.broadcasted_iota(jnp.int32, (HEAD_DIM, 1), 0)
    q1 = jnp.where(channel < QK_DIM, q_t, jnp.zeros_like(q_t))
    q2 = jnp.where(channel >= QK_DIM, q_t, jnp.zeros_like(q_t))
    _attn_pipeline(pl.program_id(1), [q1, q2], k_ref, v_ref, [st1, st2], tq, tk)

    dl = dl_ref[...]
    lam = (jnp.exp(jnp.sum(dl[0:1, :] * dl[1:2, :], axis=1, keepdims=True))
           - jnp.exp(jnp.sum(dl[2:3, :] * dl[3:4, :], axis=1, keepdims=True)) + lambda_init)
    o = st1.normalized() - lam * st2.normalized()
    o = o * lax.rsqrt(jnp.mean(o * o, axis=0, keepdims=True) + LN_EPS)
    o_ref[...] = (o.T * gain_ref[...] * (1.0 - lambda_init)).astype(BF16)


def _diff_attn(rope, segs_t, diff_lambda, diff_gain, layer, lambda_init, tq, tk):
    s = rope.shape[0]
    k_block0 = ROPE_SEGS.index(4) * SEG // HEAD_DIM
    return pl.pallas_call(
        functools.partial(_diff_attn_kernel, tq=tq, tk=tk, lambda_init=lambda_init),
        grid=(N_HEADS, s // tq),
        in_specs=[
            pl.BlockSpec((None, tq // tk, HEAD_DIM, tk), lambda h, i: (1, i, h, 0)),
            pl.BlockSpec((s, HEAD_DIM), lambda h, i: (0, k_block0 + h)),
            pl.BlockSpec((None, s // tk, HEAD_DIM, tk), lambda h, i: (2, 0, h, 0)),
            pl.BlockSpec((None, 4, QK_DIM), lambda h, i: (layer, 0, 0)),
            pl.BlockSpec((None, 1, HEAD_DIM), lambda h, i: (layer, 0, 0)),
        ],
        out_specs=pl.BlockSpec((tq, HEAD_DIM), lambda h, i: (i, h)),
        out_shape=jax.ShapeDtypeStruct((s, SEG), BF16),
        scratch_shapes=_AttnState.scratch(tq, tk) * 2,
        compiler_params=_cparams(("arbitrary", "arbitrary"), 40),
        name="diff_attn",
    )(segs_t, rope, segs_t, diff_lambda, diff_gain)


def _pool_kernel(u_ref, halo_ref, wg_ref, sc_ref, o_ref, *, tm):
    i = pl.program_id(0)
    tokens_seen = (lax.broadcasted_iota(jnp.int32, (tm, 1), 0) + i * tm + 1).astype(F32)
    for g, window in enumerate(POOL_WINDOWS):
        cols = slice(g * POOL_GROUP_DIM, (g + 1) * POOL_GROUP_DIM)
        u = u_ref[:, cols].astype(F32)
        halo = halo_ref[:, cols].astype(F32)
        halo = jnp.where(i > 0, halo, jnp.zeros_like(halo))
        ext = jnp.concatenate([halo, u], axis=0)
        shift = 1
        while shift < window:
            ext = ext + pltpu.roll(ext, shift, 0)
            shift *= 2
        pooled = ext[POOL_HALO:, :] / jnp.minimum(tokens_seen, float(window))
        delta = (pooled - u).astype(BF16)
        y = jnp.dot(delta, wg_ref[g].astype(BF16), preferred_element_type=F32) * sc_ref[:, cols]
        o_ref[:, cols] = y.astype(BF16)


def _pool(plain, w_pool_group, pool_scale, layer, tm):
    s = plain.shape[0]
    n_groups = len(POOL_WINDOWS)
    halo_blocks_per_tile = tm // POOL_HALO
    blk = PLAIN_SEGS.index(6)
    return pl.pallas_call(
        functools.partial(_pool_kernel, tm=tm),
        grid=(s // tm,),
        in_specs=[
            pl.BlockSpec((tm, SEG), lambda i: (i, blk)),
            pl.BlockSpec((POOL_HALO, SEG), lambda i: (jnp.maximum(i * halo_blocks_per_tile - 1, 0), blk)),
            pl.BlockSpec((None, n_groups, POOL_GROUP_DIM, POOL_GROUP_DIM), lambda i: (layer, 0, 0, 0)),
            pl.BlockSpec((None, 1, SEG), lambda i: (layer, 0, 0)),
        ],
        out_specs=pl.BlockSpec((tm, SEG), lambda i: (i, 0)),
        out_shape=jax.ShapeDtypeStruct((s, SEG), BF16),
        compiler_params=_cparams(("arbitrary",), 32),
        name="pool",
    )(plain, plain, w_pool_group, pool_scale)


def _merge_kernel(ya_ref, yb_ref, yc_ref, wa_ref, wb_ref, wc_ref, ga_ref, gb_ref, gc_ref, o_ref,
                  wab_ref, wbb_ref, wcb_ref):
    i = pl.program_id(1)

    def emit(ws):
        h = ga_ref[...].astype(F32) * jnp.dot(ya_ref[...], ws[0], preferred_element_type=F32)
        h = h + gb_ref[...].astype(F32) * jnp.dot(yb_ref[...], ws[1], preferred_element_type=F32)
        h = h + gc_ref[...].astype(F32) * jnp.dot(yc_ref[...], ws[2], preferred_element_type=F32)
        o_ref[...] = h.astype(BF16)

    _with_bf16_weights(i == 0, i != 0, [wa_ref, wb_ref, wc_ref], [wab_ref, wbb_ref, wcb_ref], emit)


def _merge(y_a, y_b, y_c, w_a, w_b, w_c, gates, layer, tm, tn):
    s = y_a.shape[0]
    gate_blocks = D_MODEL // tn
    y_spec = pl.BlockSpec((tm, SEG), lambda j, i: (i, 0))
    w_spec = pl.BlockSpec((None, SEG, tn), lambda j, i: (layer, 0, j))
    gate_spec = lambda br: pl.BlockSpec((tm, tn), lambda j, i: (i, br * gate_blocks + j))
    return pl.pallas_call(
        _merge_kernel,
        grid=(D_MODEL // tn, s // tm),
        in_specs=[y_spec, y_spec, y_spec, w_spec, w_spec, w_spec, gate_spec(0), gate_spec(1), gate_spec(2)],
        out_specs=pl.BlockSpec((tm, tn), lambda j, i: (i, j)),
        out_shape=jax.ShapeDtypeStruct((s, D_MODEL), BF16),
        scratch_shapes=[pltpu.VMEM((SEG, tn), BF16)] * 3,
        compiler_params=_cparams(("arbitrary", "arbitrary"), 48),
        name="merge",
    )(y_a, y_b, y_c, w_a, w_b, w_c, gates, gates, gates)


def _layer_norm_rows(z, g, b):
    mu = jnp.mean(z, axis=1, keepdims=True)
    zc = z - mu
    var = jnp.mean(zc * zc, axis=1, keepdims=True)
    return zc * lax.rsqrt(var + LN_EPS) * g + b


def _matmul_ln_kernel(h_ref, w_ref, x_ref, g_ref, b_ref, xo_ref, xb_ref, wbf_ref, *, alpha):
    i = pl.program_id(0)

    def emit(ws):
        mix = jnp.dot(h_ref[...], ws[0], preferred_element_type=F32)
        y = _layer_norm_rows(alpha * x_ref[...] + mix, g_ref[...], b_ref[...])
        xo_ref[...] = y
        xb_ref[...] = y.astype(BF16)

    _with_bf16_weights(i == 0, i != 0, [w_ref], [wbf_ref], emit)


def _matmul_ln(h, w, x, g, b, layer, alpha, tm):
    s, k = h.shape
    n = w.shape[-1]
    row_spec = pl.BlockSpec((tm, n), lambda i: (i, 0))
    par_spec = pl.BlockSpec((None, 1, n), lambda i: (layer, 0, 0))
    return pl.pallas_call(
        functools.partial(_matmul_ln_kernel, alpha=alpha),
        grid=(s // tm,),
        in_specs=[pl.BlockSpec((tm, k), lambda i: (i, 0)),
                  pl.BlockSpec((None, k, n), lambda i: (layer, 0, 0), pipeline_mode=pl.Buffered(1)),
                  row_spec, par_spec, par_spec],
        out_specs=[row_spec, row_spec],
        out_shape=[jax.ShapeDtypeStruct((s, n), F32), jax.ShapeDtypeStruct((s, n), BF16)],
        scratch_shapes=[pltpu.VMEM((k, n), BF16)],
        compiler_params=_cparams(("arbitrary",), 56),
        name="outproj_ln",
    )(h, w, x, g, b)


def _add_ln_kernel(x_ref, f_ref, g_ref, b_ref, xo_ref, xb_ref, *, alpha):
    y = _layer_norm_rows(alpha * x_ref[...] + f_ref[...], g_ref[...], b_ref[...])
    xo_ref[...] = y
    xb_ref[...] = y.astype(BF16)


def _add_ln(x, f, g, b, layer, alpha, tm):
    s, d = x.shape
    row_spec = pl.BlockSpec((tm, d), lambda i: (i, 0))
    par_spec = pl.BlockSpec((None, 1, d), lambda i: (layer, 0, 0))
    return pl.pallas_call(
        functools.partial(_add_ln_kernel, alpha=alpha),
        grid=(s // tm,),
        in_specs=[row_spec, row_spec, par_spec, par_spec],
        out_specs=[row_spec, row_spec],
        out_shape=[jax.ShapeDtypeStruct((s, d), F32), jax.ShapeDtypeStruct((s, d), BF16)],
        compiler_params=_cparams(("arbitrary",), 32),
        name="add_ln",
    )(x, f, g, b)


def _grouped_weights(tf_ref, tg_ref, tr_ref, ge_ref, meta_ref, w_hbms, wbuf, wbf_refs, sem, emit, *,
                     layer, tn, half_rows):
    j = pl.program_id(0)
    t = pl.program_id(1)
    n_used, n_groups = meta_ref[0], meta_ref[1]
    active = t < n_used
    first = tf_ref[t] == 1
    g = tg_ref[t]
    block = j * n_groups + g
    slot = block % 2

    def copies(expert, col_block, sl):
        col = pl.multiple_of(col_block * tn, tn)
        return [pltpu.make_async_copy(w.at[layer, expert, :, pl.ds(col, tn)], wbuf.at[sl, m], sem.at[sl])
                for m, w in enumerate(w_hbms)]

    @pl.when(active & first)
    def _stream():
        @pl.when(block == 0)
        def _cold_start():
            for cp in copies(ge_ref[0], 0, 0):
                cp.start()

        for cp in copies(ge_ref[g], j, slot):
            cp.wait()
        wrap = g + 1 == n_groups
        g_next = jnp.where(wrap, 0, g + 1)
        j_next = jnp.where(wrap, j + 1, j)

        @pl.when(j_next < pl.num_programs(0))
        def _prefetch():
            for cp in copies(ge_ref[g_next], j_next, 1 - slot):
                cp.start()

    half_tile = tr_ref[t] <= half_rows
    for half in (False, True):
        sel = half_tile if half else jnp.logical_not(half_tile)

        @pl.when(active & first & sel)
        def _fresh(half=half):
            ws = [wbuf[slot, m].astype(BF16) for m in range(len(w_hbms))]
            for dst, w in zip(wbf_refs, ws):
                dst[...] = w
            emit(ws, half)

        @pl.when(active & jnp.logical_not(first) & sel)
        def _cached(half=half):
            emit([r[...] for r in wbf_refs], half)

    return active


def _gate_up_kernel(tf_ref, tg_ref, tr_ref, ge_ref, meta_ref, x_ref, wg_hbm, wu_hbm, o_ref,
                    wbuf, wgb_ref, wub_ref, sem, *, layer, tn):
    half_rows = x_ref.shape[0] // 2

    def emit(ws, half):
        rows = half_rows if half else x_ref.shape[0]
        x = x_ref[0:rows, :]
        g = jnp.dot(x, ws[0], preferred_element_type=F32)
        u = jnp.dot(x, ws[1], preferred_element_type=F32)
        o_ref[0:rows, :] = (g * jax.nn.sigmoid(g) * u).astype(BF16)
        if half:
            o_ref[rows:, :] = jnp.zeros((x_ref.shape[0] - rows, o_ref.shape[1]), BF16)

    active = _grouped_weights(tf_ref, tg_ref, tr_ref, ge_ref, meta_ref, [wg_hbm, wu_hbm], wbuf, [wgb_ref, wub_ref],
                              sem, emit, layer=layer, tn=tn, half_rows=half_rows)

    @pl.when(jnp.logical_not(active))
    def _unused_tile():
        o_ref[...] = jnp.zeros_like(o_ref)


def _row_tile(j, t, tf, tg, tr, ge, meta):
    return (jnp.minimum(t, meta[0] - 1), 0)


def _out_tile(j, t, tf, tg, tr, ge, meta):
    return (t, j)


def _gate_up(groups, x_sorted, w_gate, w_up, layer, tm, tn):
    n_slots, k = x_sorted.shape
    n = w_gate.shape[-1]
    return pl.pallas_call(
        functools.partial(_gate_up_kernel, layer=layer, tn=tn),
        grid_spec=pltpu.PrefetchScalarGridSpec(
            num_scalar_prefetch=5,
            grid=(n // tn, n_slots // tm),
            in_specs=[pl.BlockSpec((tm, k), _row_tile), pl.BlockSpec(memory_space=pl.ANY),
                      pl.BlockSpec(memory_space=pl.ANY)],
            out_specs=pl.BlockSpec((tm, tn), _out_tile),
            scratch_shapes=[pltpu.VMEM((2, 2, k, tn), F32), pltpu.VMEM((k, tn), BF16), pltpu.VMEM((k, tn), BF16),
                            pltpu.SemaphoreType.DMA((2,))],
        ),
        out_shape=jax.ShapeDtypeStruct((n_slots, n), BF16),
        compiler_params=_cparams(("arbitrary", "arbitrary"), 52),
        name="ffn_gate_up",
    )(*groups, x_sorted, w_gate, w_up)


def _down_kernel(tf_ref, tg_ref, tr_ref, ge_ref, meta_ref, h_ref, w_hbm, o_ref, wbuf, wbf_ref, sem, *,
                 layer, tn):
    half_rows = h_ref.shape[0] // 2

    def emit(ws, half):
        rows = half_rows if half else h_ref.shape[0]
        o_ref[0:rows, :] = jnp.dot(h_ref[0:rows, :], ws[0], preferred_element_type=F32)
        if half:
            o_ref[rows:, :] = jnp.zeros((h_ref.shape[0] - rows, o_ref.shape[1]), F32)

    active = _grouped_weights(tf_ref, tg_ref, tr_ref, ge_ref, meta_ref, [w_hbm], wbuf, [wbf_ref], sem, emit,
                              layer=layer, tn=tn, half_rows=half_rows)

    @pl.when(jnp.logical_not(active))
    def _unused_tile():
        o_ref[...] = jnp.zeros_like(o_ref)


def _down(groups, h_sorted, w_down, layer, tm, tn):
    n_slots, k = h_sorted.shape
    n = w_down.shape[-1]
    return pl.pallas_call(
        functools.partial(_down_kernel, layer=layer, tn=tn),
        grid_spec=pltpu.PrefetchScalarGridSpec(
            num_scalar_prefetch=5,
            grid=(n // tn, n_slots // tm),
            in_specs=[pl.BlockSpec((tm, k), _row_tile), pl.BlockSpec(memory_space=pl.ANY)],
            out_specs=pl.BlockSpec((tm, tn), _out_tile),
            scratch_shapes=[pltpu.VMEM((2, 1, k, tn), F32), pltpu.VMEM((k, tn), BF16),
                            pltpu.SemaphoreType.DMA((2,))],
        ),
        out_shape=jax.ShapeDtypeStruct((n_slots, n), F32),
        compiler_params=_cparams(("arbitrary", "arbitrary"), 56),
        name="ffn_down",
    )(*groups, h_sorted, w_down)


def _single_group(n_tiles, tm):
    return (jnp.zeros((n_tiles,), jnp.int32).at[0].set(1),
            jnp.zeros((n_tiles,), jnp.int32),
            jnp.full((n_tiles,), tm, jnp.int32),
            jnp.zeros((N_EXPERTS,), jnp.int32),
            jnp.array([n_tiles, 1], jnp.int32))


def _router_kernel(x_ref, rw_ref, rb_ref, o_ref, cnt_ref, carry_ref, *, tm):
    i = pl.program_id(0)

    @pl.when(i == 0)
    def _init():
        carry_ref[...] = jnp.zeros_like(carry_ref)

    logits = jnp.dot(x_ref[...], rw_ref[...], preferred_element_type=F32,
                     precision=lax.Precision.HIGHEST) + rb_ref[...]
    lane = lax.broadcasted_iota(jnp.int32, (tm, LANES), 1).astype(F32)
    lg = jnp.where(lane < N_EXPERTS, logits, -jnp.inf)
    v1 = jnp.max(lg, axis=1, keepdims=True)
    e1 = jnp.min(jnp.where(lg == v1, lane, float(LANES)), axis=1, keepdims=True)
    lg2 = jnp.where(lane == e1, -jnp.inf, lg)
    v2 = jnp.max(lg2, axis=1, keepdims=True)
    e2 = jnp.min(jnp.where(lg2 == v2, lane, float(LANES)), axis=1, keepdims=True)
    t2 = jnp.exp(v2 - v1)
    w1 = 1.0 / (1.0 + t2)
    w2 = t2 / (1.0 + t2)
    pick1 = lane == e1
    pick2 = lane == e2
    onehot = jnp.where(pick1 | pick2, 1.0, 0.0)
    row = lax.broadcasted_iota(jnp.int32, (tm, tm), 0)
    col = lax.broadcasted_iota(jnp.int32, (tm, tm), 1)
    before = (col < row).astype(BF16)
    seen = jnp.dot(before, onehot.astype(BF16), preferred_element_type=F32) + carry_ref[...]
    r1 = jnp.sum(jnp.where(pick1, seen, 0.0), axis=1, keepdims=True)
    r2 = jnp.sum(jnp.where(pick2, seen, 0.0), axis=1, keepdims=True)
    carry_ref[...] = carry_ref[...] + jnp.sum(onehot, axis=0, keepdims=True)
    o_ref[...] = jnp.where(lane == 0, e1, jnp.where(lane == 1, e2, jnp.where(lane == 2, w1,
                 jnp.where(lane == 3, w2, jnp.where(lane == 4, r1, jnp.where(lane == 5, r2, 0.0))))))
    cnt_ref[...] = carry_ref[...]


def _router(x, rw_pad, rb_pad, tm):
    s, d = x.shape
    return pl.pallas_call(
        functools.partial(_router_kernel, tm=tm),
        grid=(s // tm,),
        in_specs=[pl.BlockSpec((tm, d), lambda i: (i, 0)),
                  pl.BlockSpec((d, LANES), lambda i: (0, 0)),
                  pl.BlockSpec((1, LANES), lambda i: (0, 0))],
        out_specs=[pl.BlockSpec((tm, LANES), lambda i: (i, 0)), pl.BlockSpec((1, LANES), lambda i: (0, 0))],
        out_shape=[jax.ShapeDtypeStruct((s, LANES), F32), jax.ShapeDtypeStruct((1, LANES), F32)],
        scratch_shapes=[pltpu.VMEM((1, LANES), F32)],
        compiler_params=_cparams(("arbitrary",), 32),
        name="router",
    )(x, rw_pad, rb_pad)


def _scatter_copies(x_ref, o_hbm, p1_ref, p2_ref, base, r, sem):
    return (pltpu.make_async_copy(x_ref.at[r], o_hbm.at[p1_ref[base + r]], sem),
            pltpu.make_async_copy(x_ref.at[r], o_hbm.at[p2_ref[base + r]], sem))


def _dispatch_kernel(p1_ref, p2_ref, x_ref, init_hbm, o_hbm, sem, *, tt):
    del init_hbm
    base = pl.program_id(0) * tt

    def issue(r, carry):
        for cp in _scatter_copies(x_ref, o_hbm, p1_ref, p2_ref, base, r, sem):
            cp.start()
        return carry

    lax.fori_loop(0, tt, issue, 0, unroll=8)

    def drain(r, carry):
        for cp in _scatter_copies(x_ref, o_hbm, p1_ref, p2_ref, base, r, sem):
            cp.wait()
        return carry

    lax.fori_loop(0, tt, drain, 0, unroll=8)


def _dispatch(pos1, pos2, x_bf, n_slots, tt):
    s, d = x_bf.shape
    x_slabs = x_bf.reshape(s, d // LANES, LANES)
    out = pl.pallas_call(
        functools.partial(_dispatch_kernel, tt=tt),
        grid_spec=pltpu.PrefetchScalarGridSpec(
            num_scalar_prefetch=2,
            grid=(s // tt,),
            in_specs=[pl.BlockSpec((tt, d // LANES, LANES), lambda i, p1, p2: (i, 0, 0)),
                      pl.BlockSpec(memory_space=pl.ANY)],
            out_specs=pl.BlockSpec(memory_space=pl.ANY),
            scratch_shapes=[pltpu.SemaphoreType.DMA],
        ),
        out_shape=jax.ShapeDtypeStruct((n_slots, d // LANES, LANES), BF16),
        input_output_aliases={3: 0},
        compiler_params=_cparams(("arbitrary",), 32),
        name="moe_dispatch",
    )(pos1, pos2, x_slabs, jnp.zeros((n_slots, d // LANES, LANES), BF16))
    return out.reshape(n_slots, d)


def _row_copy(src_hbm, src_row, buf, dst_row, sem):
    return pltpu.make_async_copy(src_hbm.at[pl.ds(src_row, 1), :], buf.at[pl.ds(dst_row, 1), :], sem)


def _combine_ln_kernel(p1_ref, p2_ref, y_hbm, x_ref, route_ref, g_ref, b_ref, xo_ref, xb_ref,
                       buf, sem, *, tc, alpha):
    i = pl.program_id(0)

    def copies(step, slot, r):
        t = step * tc + r
        return (_row_copy(y_hbm, p1_ref[t], buf.at[slot, 0], r, sem.at[slot]),
                _row_copy(y_hbm, p2_ref[t], buf.at[slot, 1], r, sem.at[slot]))

    def issue(step, slot):
        def body(r, carry):
            for cp in copies(step, slot, r):
                cp.start()
            return carry

        lax.fori_loop(0, tc, body, 0, unroll=8)

    @pl.when(i == 0)
    def _first():
        issue(0, 0)

    @pl.when(i + 1 < pl.num_programs(0))
    def _prefetch():
        issue(i + 1, (i + 1) % 2)

    slot = i % 2

    def drain(r, carry):
        for cp in copies(i, slot, r):
            cp.wait()
        return carry

    lax.fori_loop(0, tc, drain, 0, unroll=8)
    route = route_ref[...]
    f = route[:, 2:3] * buf[slot, 0] + route[:, 3:4] * buf[slot, 1]
    y = _layer_norm_rows(alpha * x_ref[...] + f, g_ref[...], b_ref[...])
    xo_ref[...] = y
    xb_ref[...] = y.astype(BF16)


def _combine_ln(pos1, pos2, y_sorted, x, route, g, b, layer, alpha, tc):
    s, d = x.shape
    row_spec = pl.BlockSpec((tc, d), lambda i, p1, p2: (i, 0))
    par_spec = pl.BlockSpec((None, 1, d), lambda i, p1, p2: (layer, 0, 0))
    return pl.pallas_call(
        functools.partial(_combine_ln_kernel, tc=tc, alpha=alpha),
        grid_spec=pltpu.PrefetchScalarGridSpec(
            num_scalar_prefetch=2,
            grid=(s // tc,),
            in_specs=[pl.BlockSpec(memory_space=pl.ANY), row_spec,
                      pl.BlockSpec((tc, LANES), lambda i, p1, p2: (i, 0)), par_spec, par_spec],
            out_specs=[row_spec, row_spec],
            scratch_shapes=[pltpu.VMEM((2, 2, tc, d), F32), pltpu.SemaphoreType.DMA((2,))],
        ),
        out_shape=[jax.ShapeDtypeStruct((s, d), F32), jax.ShapeDtypeStruct((s, d), BF16)],
        compiler_params=_cparams(("arbitrary",), 32),
        name="moe_combine_ln",
    )(pos1, pos2, y_sorted, x, route, g, b)


def _moe_plan(route, counts, tm, n_tiles):
    expert = route[:, 0:2].astype(jnp.int32)
    rank = route[:, 4:6].astype(jnp.int32)
    count = counts[0, :N_EXPERTS].astype(jnp.int32)
    padded = ((count + tm - 1) // tm) * tm
    ends = jnp.cumsum(padded)
    starts = ends - padded
    pos = starts[expert] + rank
    n_used = ends[-1] // tm
    tile = jnp.arange(n_tiles, dtype=jnp.int32)
    last = jnp.minimum(tile, n_used - 1)
    tile_expert = jnp.sum(last[:, None] * tm >= ends[None, :], axis=1).astype(jnp.int32)
    prev = jnp.concatenate([jnp.full((1,), -1, jnp.int32), tile_expert[:-1]])
    tile_first = (tile_expert != prev).astype(jnp.int32)
    nonempty = count > 0
    group_of_expert = jnp.cumsum(nonempty.astype(jnp.int32)) - 1
    experts = jnp.arange(N_EXPERTS, dtype=jnp.int32)
    group_expert = jnp.argsort(jnp.where(nonempty, experts, experts + N_EXPERTS)).astype(jnp.int32)
    meta = jnp.stack([n_used, jnp.sum(nonempty)]).astype(jnp.int32)
    tile_rows = jnp.clip(count[tile_expert] - (last * tm - starts[tile_expert]), 0, tm).astype(jnp.int32)
    return pos, (tile_first, group_of_expert[tile_expert], tile_rows, group_expert, meta)


def _rope_tables(seq):
    pos = jnp.arange(seq, dtype=F32)
    inv = ROPE_THETA ** (-jnp.arange(0, QK_DIM, 2, dtype=F32) / QK_DIM)
    ang = pos[:, None] * inv[None, :]
    cos, sin = jnp.cos(ang), jnp.sin(ang)
    return jnp.tile(cos, (1, 4)), jnp.tile(jnp.concatenate([-sin, sin], axis=1), (1, 2))


def _pad_lanes(a):
    return jnp.pad(a, ((0, 0), (0, LANES - a.shape[1])))


def kernel(x, w_in, b_forget, w_pool_group, pool_scale, diff_lambda, diff_norm_gain, w_branch_a, w_branch_b, w_branch_c, b_gate, w_out, ln1_g, ln1_b, ln2_g, ln2_b, ffn_w_gate, ffn_w_up, ffn_w_down, router_w, router_b, expert_w_gate, expert_w_up, expert_w_down):
    batch, s, d = x.shape
    assert batch == 1 and d == D_MODEL and w_in.shape[-1] == N_MAIN + N_HEADS
    depth = w_in.shape[0]
    alpha = (2 * depth) ** 0.25

    tm = min(1024, s)
    tq = min(1024, s)
    tk = tq // 4
    te = min(512, s)
    tg = min(256, s)
    n_ff = 512
    n_tiles = 2 * s // te + N_EXPERTS
    n_slots = n_tiles * te

    cos_t, sin_t = _rope_tables(s)
    w_in_t = jnp.swapaxes(w_in, 1, 2)
    b_gate_flat = b_gate.reshape(depth, 1, 3 * D_MODEL)
    pool_scale3 = pool_scale.reshape(depth, 1, SEG)
    gain3 = diff_norm_gain.reshape(depth, 1, HEAD_DIM)
    ln = [p.reshape(depth, 1, D_MODEL) for p in (ln1_g, ln1_b, ln2_g, ln2_b)]
    dense_w = [w.reshape(w.shape[0], 1, *w.shape[1:]) for w in (ffn_w_gate, ffn_w_up, ffn_w_down)]

    xf = x.reshape(s, d)
    xb = xf.astype(BF16)
    for layer in range(depth):
        lambda_init = 0.8 - 0.6 * math.exp(-0.3 * layer)
        plain, rope, gates = _inproj_all(xb, w_in_t, b_gate_flat, cos_t, sin_t, layer, tm)
        b_fg = _pad_lanes(b_forget[layer].reshape(1, N_HEADS))
        q_aug_t, k_aug = _forget_prep(xb, w_in_t, b_fg, plain, layer, te, tq)
        segs_t = _transpose_segs(plain, rope, te, tk)
        y_a = _fox_attn(q_aug_t, k_aug, segs_t, tq, tk)
        y_b = _pool(plain, w_pool_group, pool_scale3, layer, tm)
        y_c = _diff_attn(rope, segs_t, diff_lambda, gain3, layer, lambda_init, tq, tk)
        h = _merge(y_a, y_b, y_c, w_branch_a, w_branch_b, w_branch_c, gates, layer, tm, 512)
        xf, xb = _matmul_ln(h, w_out, xf, ln[0], ln[1], layer, alpha, tg)

        j = layer // 2
        if layer % 2 == 0:
            hidden = _gate_up(_single_group(s // tm, tm), xb, dense_w[0], dense_w[1], j, tm, n_ff)
            f = _down(_single_group(s // te, te), hidden, dense_w[2], j, te, 512)
            xf, xb = _add_ln(xf, f, ln[2], ln[3], layer, alpha, tg)
        else:
            route, counts = _router(xf, _pad_lanes(router_w[j]), _pad_lanes(router_b[j].reshape(1, N_EXPERTS)), te)
            pos, groups = _moe_plan(route, counts, te, n_tiles)
            x_sorted = _dispatch(pos[:, 0], pos[:, 1], xb, n_slots, te)
            hidden = _gate_up(groups, x_sorted, expert_w_gate, expert_w_up, j, te, n_ff)
            y_sorted = _down(groups, hidden, expert_w_down, j, te, 512)
            xf, xb = _combine_ln(pos[:, 0], pos[:, 1], y_sorted, xf, route, ln[2], ln[3], layer, alpha, tg)
    return xf.reshape(batch, s, d)
```
